```python
import math
import jax, jax.numpy as jnp
from jax import lax
import numpy as np


D_MODEL = 2048
BATCH = 2
SEQ = 4096
DEPTH = 1

PLE_DIM = 256
D_FF = 5632
CONV_WIDTH = D_MODEL
SC_KERNEL = 3
SSD_HEADS = 32
SSD_HEAD_DIM = 64
SSD_WIDTH = SSD_HEADS * SSD_HEAD_DIM
SSD_STATE = 128
SSD_GROUPS = 4
SSD_CONV = 4
CHUNK = 128
SSD_BC = SSD_GROUPS * SSD_STATE
SSD_XBC = SSD_WIDTH + 2 * SSD_BC
MIX_WIDTH = CONV_WIDTH + SSD_WIDTH
IN_COLS = 3 * CONV_WIDTH + SSD_WIDTH + SSD_XBC + SSD_HEADS
EPS = 1e-6

kernel_name = 'hymba_shortconv_ssd_macaron_ple'


def rmsnorm(x, w):
    xf = x.astype(jnp.float32)
    y = xf * lax.rsqrt(jnp.mean(xf * xf, axis=-1, keepdims=True) + EPS)
    return (y * w.astype(jnp.float32)).astype(x.dtype)


def gated_group_rmsnorm(y, z, w):
    g = (y * jax.nn.silu(z.astype(jnp.float32)))
    shp = g.shape
    g = g.reshape(shp[:-1] + (SSD_GROUPS, shp[-1] // SSD_GROUPS))
    g = g * lax.rsqrt(jnp.mean(g * g, axis=-1, keepdims=True) + EPS)
    return g.reshape(shp) * w.astype(jnp.float32)


def causal_dwconv(x, w):
    k = w.shape[0]
    c = x.shape[-1]
    return lax.conv_general_dilated(
        x, w[:, None, :].astype(x.dtype), window_strides=(1,), padding=[(k - 1, 0)],
        dimension_numbers=('NWC', 'WIO', 'NWC'), feature_group_count=c)


def swiglu(x, w_in, w_out):
    g, u = jnp.split(x @ w_in, 2, axis=-1)
    return (jax.nn.silu(g) * u) @ w_out


def segsum_exp(a_cs):
    n = a_cs.shape[-1]
    mask = jnp.tril(jnp.ones((n, n), dtype=bool))
    diff = a_cs[..., :, None] - a_cs[..., None, :]
    return jnp.where(mask, jnp.exp(jnp.where(mask, diff, 0.0)), 0.0)


def ssd_chunked(x, dt, a, bm, cm):
    b, L = x.shape[0], x.shape[1]
    nc = L // CHUNK
    r = SSD_HEADS // SSD_GROUPS
    X = (x * dt[..., None]).reshape(b, nc, CHUNK, SSD_GROUPS, r, SSD_HEAD_DIM)
    A = (dt * a).reshape(b, nc, CHUNK, SSD_GROUPS, r).transpose(0, 1, 3, 4, 2)
    Bc = bm.reshape(b, nc, CHUNK, SSD_GROUPS, SSD_STATE)
    Cc = cm.reshape(b, nc, CHUNK, SSD_GROUPS, SSD_STATE)
    a_cs = jnp.cumsum(A, axis=-1)
    Lmat = segsum_exp(a_cs)
    CB = jnp.einsum('bclgn,bcsgn->bcgls', Cc, Bc)
    y_diag = jnp.einsum('bcgls,bcgrls,bcsgrp->bclgrp', CB, Lmat, X)
    decay_states = jnp.exp(a_cs[..., -1:] - a_cs)
    states = jnp.einsum('bclgn,bcgrl,bclgrp->bcgrpn', Bc, decay_states, X)
    chunk_decay = jnp.exp(a_cs[..., -1])

    def step(s, inp):
        st, dec = inp
        return s * dec[..., None, None] + st, s

    s0 = jnp.zeros((b, SSD_GROUPS, r, SSD_HEAD_DIM, SSD_STATE), jnp.float32)
    _, prev = lax.scan(step, s0, (states.transpose(1, 0, 2, 3, 4, 5), chunk_decay.transpose(1, 0, 2, 3)))
    prev = prev.transpose(1, 0, 2, 3, 4, 5)
    y_off = jnp.einsum('bclgn,bcgrpn,bcgrl->bclgrp', Cc, prev, jnp.exp(a_cs))
    return (y_diag + y_off).reshape(b, L, SSD_HEADS, SSD_HEAD_DIM)


def hybrid_mixer(u, w_in, sc_conv_w, ssd_conv_w, ssd_conv_b, dt_bias, a_log, d_skip, ssd_norm_w, w_out):
    b, L, _ = u.shape
    zall = u @ w_in
    cuts = [CONV_WIDTH, 2 * CONV_WIDTH, 3 * CONV_WIDTH, 3 * CONV_WIDTH + SSD_WIDTH,
            3 * CONV_WIDTH + SSD_WIDTH + SSD_XBC]
    sc_b, sc_c, sc_x, z, xbc, dt_raw = jnp.split(zall, cuts, axis=-1)
    y_a = sc_b * causal_dwconv(sc_c * sc_x, sc_conv_w)
    xbc = jax.nn.silu(causal_dwconv(xbc, ssd_conv_w) + ssd_conv_b.astype(xbc.dtype))
    xs, bm, cm = jnp.split(xbc, [SSD_WIDTH, SSD_WIDTH + SSD_BC], axis=-1)
    f32 = jnp.float32
    dt = jax.nn.softplus(dt_raw.astype(f32) + dt_bias.astype(f32))
    a = -jnp.exp(a_log.astype(f32))
    x4 = xs.astype(f32).reshape(b, L, SSD_HEADS, SSD_HEAD_DIM)
    y = ssd_chunked(x4, dt, a,
                    bm.astype(f32).reshape(b, L, SSD_GROUPS, SSD_STATE),
                    cm.astype(f32).reshape(b, L, SSD_GROUPS, SSD_STATE))
    y = (y + d_skip.astype(f32)[:, None] * x4).reshape(b, L, SSD_WIDTH)
    y_b = gated_group_rmsnorm(y, z, ssd_norm_w).astype(u.dtype)
    return jnp.concatenate([y_a, y_b], axis=-1) @ w_out


def setup_inputs(seed: int = 0) -> dict:
    key = jax.random.key(seed)
    ks = jax.random.split(key, 24)
    f32 = jnp.float32

    def nrm(k, shape, fan_in):
        return jax.random.normal(k, shape, f32) * (fan_in ** -0.5)

    def gain(k, shape):
        return 1.0 + 0.02 * jax.random.normal(k, shape, f32)

    dt0 = jnp.exp(jax.random.uniform(ks[10], (DEPTH, SSD_HEADS), f32, math.log(1e-3), math.log(1e-1)))
    return {
        'x': jax.random.normal(ks[0], (BATCH, SEQ, D_MODEL), f32),
        'p': jax.random.normal(ks[1], (DEPTH, BATCH, SEQ, PLE_DIM), f32),
        'ffn1_norm': gain(ks[2], (DEPTH, D_MODEL)),
        'ffn1_w_in': nrm(ks[3], (DEPTH, D_MODEL, 2 * D_FF), D_MODEL),
        'ffn1_w_out': nrm(ks[4], (DEPTH, D_FF, D_MODEL), D_FF),
        'mix_norm': gain(ks[5], (DEPTH, D_MODEL)),
        'mix_w_in': nrm(ks[6], (DEPTH, D_MODEL, IN_COLS), D_MODEL),
        'sc_conv_w': nrm(ks[7], (DEPTH, SC_KERNEL, CONV_WIDTH), SC_KERNEL),
        'ssd_conv_w': nrm(ks[8], (DEPTH, SSD_CONV, SSD_XBC), SSD_CONV),
        'ssd_conv_b': 0.02 * jax.random.normal(ks[9], (DEPTH, SSD_XBC), f32),
        'ssd_dt_bias': dt0 + jnp.log(-jnp.expm1(-dt0)),
        'ssd_a_log': jnp.log(jax.random.uniform(ks[11], (DEPTH, SSD_HEADS), f32, 1.0, 16.0)),
        'ssd_d': gain(ks[12], (DEPTH, SSD_HEADS)),
        'ssd_norm': gain(ks[13], (DEPTH, SSD_WIDTH)),
        'mix_w_out': nrm(ks[14], (DEPTH, MIX_WIDTH, D_MODEL), MIX_WIDTH),
        'ffn2_norm': gain(ks[15], (DEPTH, D_MODEL)),
        'ffn2_w_in': nrm(ks[16], (DEPTH, D_MODEL, 2 * D_FF), D_MODEL),
        'ffn2_w_out': nrm(ks[17], (DEPTH, D_FF, D_MODEL), D_FF),
        'ple_norm': gain(ks[18], (DEPTH, D_MODEL)),
        'ple_w_gate': nrm(ks[19], (DEPTH, D_MODEL, D_MODEL), D_MODEL),
        'ple_w_proj': nrm(ks[20], (DEPTH, PLE_DIM, D_MODEL), PLE_DIM),
        'final_norm': gain(ks[21], (D_MODEL,)),
    }


def reference(x, p, ffn1_norm, ffn1_w_in, ffn1_w_out, mix_norm, mix_w_in, sc_conv_w, ssd_conv_w,
              ssd_conv_b, ssd_dt_bias, ssd_a_log, ssd_d, ssd_norm, mix_w_out, ffn2_norm, ffn2_w_in,
              ffn2_w_out, ple_norm, ple_w_gate, ple_w_proj, final_norm):
    h = x
    for i in range(DEPTH):
        h = h + 0.5 * swiglu(rmsnorm(h, ffn1_norm[i]), ffn1_w_in[i], ffn1_w_out[i])
        h = h + hybrid_mixer(rmsnorm(h, mix_norm[i]), mix_w_in[i], sc_conv_w[i], ssd_conv_w[i],
                             ssd_conv_b[i], ssd_dt_bias[i], ssd_a_log[i], ssd_d[i], ssd_norm[i],
                             mix_w_out[i])
        h = h + 0.5 * swiglu(rmsnorm(h, ffn2_norm[i]), ffn2_w_in[i], ffn2_w_out[i])
        gate = jax.nn.sigmoid(rmsnorm(h, ple_norm[i]) @ ple_w_gate[i])
        h = h + gate * (p[i] @ ple_w_proj[i])
    return rmsnorm(h, final_norm)
```

```python
import functools

import jax
import jax.numpy as jnp
from jax import lax
from jax.experimental import pallas as pl
from jax.experimental.pallas import tpu as pltpu

F32 = jnp.float32
BF16 = jnp.bfloat16

D_MODEL = 2048
D_FF = 5632
CONV_WIDTH = 2048
SC_KERNEL = 3
SSD_HEADS = 32
SSD_HEAD_DIM = 64
SSD_WIDTH = SSD_HEADS * SSD_HEAD_DIM
SSD_STATE = 128
SSD_GROUPS = 4
SSD_CONV = 4
CHUNK = 128
SSD_BC = SSD_GROUPS * SSD_STATE
SSD_XBC = SSD_WIDTH + 2 * SSD_BC
MIX_WIDTH = CONV_WIDTH + SSD_WIDTH
MAIN_COLS = 3 * CONV_WIDTH + SSD_WIDTH + SSD_XBC
EPS = 1e-6

LANES = 128
SUBLANES = 8
VMEM_LIMIT_BYTES = 60000 * 1024
GROUP_W = SSD_WIDTH // SSD_GROUPS
HEADS_PER_GROUP = SSD_HEADS // SSD_GROUPS
QUAD = 4
QUAD_W = QUAD * SSD_HEAD_DIM


def _params(n_axes):
    return pltpu.CompilerParams(
        dimension_semantics=("arbitrary",) * n_axes,
        vmem_limit_bytes=VMEM_LIMIT_BYTES)


def _rmsnorm(x, w):
    return x * lax.rsqrt(jnp.mean(x * x, axis=-1, keepdims=True) + EPS) * w


def _silu(x):
    return x * jax.nn.sigmoid(x)


def _ffn_body(x_ref, nw_ref, wg_ref, wu_ref, wo_ref, o_ref, xn_ref):
    j = pl.program_id(1)

    @pl.when(j == 0)
    def _():
        xn_ref[...] = _rmsnorm(x_ref[...], nw_ref[...]).astype(BF16)
        o_ref[...] = jnp.zeros_like(o_ref)

    xn = xn_ref[...]
    g = jnp.dot(xn, wg_ref[...], preferred_element_type=F32)
    u = jnp.dot(xn, wu_ref[...], preferred_element_type=F32)
    a = (_silu(g) * u).astype(BF16)
    o_ref[...] += jnp.dot(a, wo_ref[...], preferred_element_type=F32)

    @pl.when(j == pl.num_programs(1) - 1)
    def _():
        o_ref[...] = x_ref[...] + 0.5 * o_ref[...]


def _ffn(h, norm_w, w_in, w_out, *, tm, tf, name):
    t = h.shape[0]
    nj = D_FF // tf
    return pl.pallas_call(
        _ffn_body,
        grid=(t // tm, nj),
        in_specs=[
            pl.BlockSpec((tm, D_MODEL), lambda i, j: (i, 0)),
            pl.BlockSpec((1, D_MODEL), lambda i, j: (0, 0)),
            pl.BlockSpec((D_MODEL, tf), lambda i, j: (0, j)),
            pl.BlockSpec((D_MODEL, tf), lambda i, j: (0, j + nj)),
            pl.BlockSpec((tf, D_MODEL), lambda i, j: (j, 0)),
        ],
        out_specs=pl.BlockSpec((tm, D_MODEL), lambda i, j: (i, 0)),
        out_shape=jax.ShapeDtypeStruct((t, D_MODEL), F32),
        scratch_shapes=[pltpu.VMEM((tm, D_MODEL), BF16)],
        compiler_params=_params(2),
        name=name,
    )(h, norm_w, w_in, w_in, w_out)


def _mix_in_body(x_ref, nw_ref, w_ref, wdt_ref, z_ref, dt_ref, xn_ref):
    j = pl.program_id(1)

    @pl.when(j == 0)
    def _():
        xn = _rmsnorm(x_ref[...], nw_ref[...]).astype(BF16)
        xn_ref[...] = xn
        dt_ref[...] = jnp.dot(xn, wdt_ref[...], preferred_element_type=F32)

    z_ref[...] = jnp.dot(xn_ref[...], w_ref[...], preferred_element_type=F32)


def _mix_in(h, norm_w, w_main, w_dt, *, tm, tn):
    t = h.shape[0]
    return pl.pallas_call(
        _mix_in_body,
        grid=(t // tm, MAIN_COLS // tn),
        in_specs=[
            pl.BlockSpec((tm, D_MODEL), lambda i, j: (i, 0)),
            pl.BlockSpec((1, D_MODEL), lambda i, j: (0, 0)),
            pl.BlockSpec((D_MODEL, tn), lambda i, j: (0, j)),
            pl.BlockSpec((D_MODEL, LANES), lambda i, j: (0, 0)),
        ],
        out_specs=[
            pl.BlockSpec((tm, tn), lambda i, j: (i, j)),
            pl.BlockSpec((tm, LANES), lambda i, j: (i, 0)),
        ],
        out_shape=[
            jax.ShapeDtypeStruct((t, MAIN_COLS), F32),
            jax.ShapeDtypeStruct((t, LANES), F32),
        ],
        scratch_shapes=[pltpu.VMEM((tm, D_MODEL), BF16)],
        compiler_params=_params(2),
        name="mix_in",
    )(h, norm_w, w_main, w_dt)


def _split3(a):
    a1 = a.astype(BF16)
    r1 = a - a1.astype(F32)
    a2 = r1.astype(BF16)
    a3 = (r1 - a2.astype(F32)).astype(BF16)
    return a1, a2, a3


def _dot_exact_rhs(a, sel):
    out = None
    for part in _split3(a):
        d = jnp.dot(part, sel, preferred_element_type=F32)
        out = d if out is None else out + d
    return out


def _dot_exact_lhs(sel, a):
    out = None
    for part in _split3(a):
        d = jnp.dot(sel, part, preferred_element_type=F32)
        out = d if out is None else out + d
    return out


def _causal_conv(buf_ref, w_ref, n_taps):
    acc = None
    for k in range(n_taps):
        tap = buf_ref[pl.ds(SUBLANES - (n_taps - 1) + k, CHUNK), :] * w_ref[k:k + 1, :]
        acc = tap if acc is None else acc + tap
    return acc


def _mixer_body(scb_ref, scc_ref, scx_ref, z_ref, xs_ref, bc_ref, dt_ref,
                scw_ref, cwx_ref, cwbc_ref, cbx_ref, cbbc_ref, dtb_ref, alog_ref,
                dsk_ref, nw_ref, e64_ref, e128_ref,
                y_ref, pbuf, xbuf, bcbuf, st_ref):
    @pl.when(pl.program_id(1) == 0)
    def _():
        pbuf[0:SUBLANES, :] = jnp.zeros((SUBLANES, CONV_WIDTH), F32)
        xbuf[0:SUBLANES, :] = jnp.zeros((SUBLANES, SSD_WIDTH), F32)
        bcbuf[0:SUBLANES, :] = jnp.zeros((SUBLANES, 2 * SSD_BC), F32)
        st_ref[...] = jnp.zeros_like(st_ref)

    body = pl.ds(SUBLANES, CHUNK)
    tail = pl.ds(CHUNK, SUBLANES)

    pbuf[body, :] = scc_ref[...] * scx_ref[...]
    y_a = scb_ref[...] * _causal_conv(pbuf, scw_ref, SC_KERNEL)
    y_ref[:, 0:CONV_WIDTH] = y_a.astype(BF16)
    pbuf[0:SUBLANES, :] = pbuf[tail, :]

    xbuf[body, :] = xs_ref[...]
    bcbuf[body, :] = bc_ref[...]
    xc = _silu(_causal_conv(xbuf, cwx_ref, SSD_CONV) + cbx_ref[...])
    bcc = _silu(_causal_conv(bcbuf, cwbc_ref, SSD_CONV) + cbbc_ref[...])
    xbuf[0:SUBLANES, :] = xbuf[tail, :]
    bcbuf[0:SUBLANES, :] = bcbuf[tail, :]

    dt_in = dt_ref[...] + dtb_ref[...]
    dtv = jnp.maximum(dt_in, 0.0) + jnp.log1p(jnp.exp(-jnp.abs(dt_in)))
    a_neg = -jnp.exp(alog_ref[...])
    row = lax.broadcasted_iota(jnp.int32, (CHUNK, CHUNK), 0)
    col = lax.broadcasted_iota(jnp.int32, (CHUNK, CHUNK), 1)
    causal = col <= row
    acs = _dot_exact_lhs(causal.astype(BF16), dtv * a_neg)
    acs_t = acs.T

    e64 = e64_ref[...]
    acs_e = _dot_exact_rhs(acs, e64)
    dt_e = _dot_exact_rhs(dtv, e64)
    acs_w = _dot_exact_rhs(acs, e128_ref[...])

    x_dt = xc * dt_e
    x_dt_b = x_dt.astype(BF16)
    acs_last = acs_e[CHUNK - 1:CHUNK, :]
    x_dec_b = (x_dt * jnp.exp(acs_last - acs_e)).astype(BF16)
    exp_acs_e = jnp.exp(acs_e)
    chunk_decay = jnp.exp(acs_last)

    lane_head = lax.broadcasted_iota(jnp.int32, (CHUNK, QUAD_W), 1) // SSD_HEAD_DIM
    y_groups = []
    for g in range(SSD_GROUPS):
        gsl = slice(g * GROUP_W, (g + 1) * GROUP_W)
        b_g = bcc[:, g * SSD_STATE:(g + 1) * SSD_STATE]
        c_g = bcc[:, SSD_BC + g * SSD_STATE:SSD_BC + (g + 1) * SSD_STATE]
        b_gb = b_g.astype(BF16)
        c_gb = c_g.astype(BF16)
        cb = lax.dot_general(c_gb, b_gb, (((1,), (1,)), ((), ())),
                             preferred_element_type=F32)
        states = jnp.dot(b_g.T.astype(BF16), x_dec_b[:, gsl], preferred_element_type=F32)
        prev = st_ref[:, gsl]
        y_off = jnp.dot(c_gb, prev.astype(BF16), preferred_element_type=F32) * exp_acs_e[:, gsl]
        st_ref[:, gsl] = prev * chunk_decay[:, gsl] + states
        y_diag = []
        for q in range(HEADS_PER_GROUP // QUAD):
            h0 = g * HEADS_PER_GROUP + q * QUAD
            m_parts = []
            x_parts = []
            x_q = x_dt_b[:, h0 * SSD_HEAD_DIM:h0 * SSD_HEAD_DIM + QUAD_W]
            for hh in range(QUAD):
                h = h0 + hh
                diff = acs_w[:, h * CHUNK:(h + 1) * CHUNK] - acs_t[h:h + 1, :]
                lmat = jnp.where(causal, jnp.exp(jnp.where(causal, diff, 0.0)), 0.0)
                m_parts.append((cb * lmat).astype(BF16))
                x_parts.append(jnp.where(lane_head == hh, x_q, jnp.zeros_like(x_q)))
            m_q = jnp.concatenate(m_parts, axis=1)
            x_bd = jnp.concatenate(x_parts, axis=0)
            y_diag.append(jnp.dot(m_q, x_bd, preferred_element_type=F32))
        y_groups.append(jnp.concatenate(y_diag, axis=1) + y_off)
    y = jnp.concatenate(y_groups, axis=1) + dsk_ref[...] * xc

    gz = y * _silu(z_ref[...])
    nw = nw_ref[...]
    for g in range(SSD_GROUPS):
        gsl = slice(g * GROUP_W, (g + 1) * GROUP_W)
        gg = gz[:, gsl]
        yb = gg * lax.rsqrt(jnp.mean(gg * gg, axis=-1, keepdims=True) + EPS) * nw[:, gsl]
        y_ref[:, CONV_WIDTH + g * GROUP_W:CONV_WIDTH + (g + 1) * GROUP_W] = yb.astype(BF16)


def _mixer(zall, dt_raw, sc_w, cw_x, cw_bc, cb_x, cb_bc, dt_bias, a_log, d_skip, norm_w,
           e64, e128, *, batch, seq):
    t = zall.shape[0]
    nc = seq // CHUNK
    wide = CONV_WIDTH // 1
    bc_blk = (3 * CONV_WIDTH + 2 * SSD_WIDTH) // (2 * SSD_BC)

    def zcol(c):
        return pl.BlockSpec((CHUNK, wide), lambda b, i, c=c: (b * nc + i, c))

    def full(shape):
        return pl.BlockSpec(shape, lambda b, i: (0,) * len(shape))

    return pl.pallas_call(
        _mixer_body,
        grid=(batch, nc),
        in_specs=[
            zcol(0), zcol(1), zcol(2), zcol(3), zcol(4),
            pl.BlockSpec((CHUNK, 2 * SSD_BC), lambda b, i: (b * nc + i, bc_blk)),
            pl.BlockSpec((CHUNK, LANES), lambda b, i: (b * nc + i, 0)),
            full((SC_KERNEL, CONV_WIDTH)),
            full((SSD_CONV, SSD_WIDTH)),
            full((SSD_CONV, 2 * SSD_BC)),
            full((1, SSD_WIDTH)),
            full((1, 2 * SSD_BC)),
            full((1, LANES)),
            full((1, LANES)),
            full((1, SSD_WIDTH)),
            full((1, SSD_WIDTH)),
            full((LANES, SSD_WIDTH)),
            full((LANES, SSD_HEADS * CHUNK)),
        ],
        out_specs=pl.BlockSpec((CHUNK, MIX_WIDTH), lambda b, i: (b * nc + i, 0)),
        out_shape=jax.ShapeDtypeStruct((t, MIX_WIDTH), BF16),
        scratch_shapes=[
            pltpu.VMEM((CHUNK + SUBLANES, CONV_WIDTH), F32),
            pltpu.VMEM((CHUNK + SUBLANES, SSD_WIDTH), F32),
            pltpu.VMEM((CHUNK + SUBLANES, 2 * SSD_BC), F32),
            pltpu.VMEM((SSD_STATE, SSD_WIDTH), F32),
        ],
        compiler_params=_params(2),
        name="mixer",
    )(zall, zall, zall, zall, zall, zall, dt_raw,
      sc_w, cw_x, cw_bc, cb_x, cb_bc, dt_bias, a_log, d_skip, norm_w, e64, e128)


def _out_proj_body(y_ref, w_ref, h_ref, o_ref):
    o_ref[...] = h_ref[...] + jnp.dot(y_ref[...], w_ref[...], preferred_element_type=F32)


def _out_proj(ycat, w_out, h, *, tm, tn):
    t = h.shape[0]
    return pl.pallas_call(
        _out_proj_body,
        grid=(t // tm, D_MODEL // tn),
        in_specs=[
            pl.BlockSpec((tm, MIX_WIDTH), lambda i, j: (i, 0)),
            pl.BlockSpec((MIX_WIDTH, tn), lambda i, j: (0, j)),
            pl.BlockSpec((tm, tn), lambda i, j: (i, j)),
        ],
        out_specs=pl.BlockSpec((tm, tn), lambda i, j: (i, j)),
        out_shape=jax.ShapeDtypeStruct((t, D_MODEL), F32),
        compiler_params=_params(2),
        name="out_proj",
    )(ycat, w_out, h)


def _ple_final_body(h_ref, p_ref, pnw_ref, wg_ref, wp_ref, fnw_ref, o_ref):
    h = h_ref[...]
    xn = _rmsnorm(h, pnw_ref[...]).astype(BF16)
    gate = jax.nn.sigmoid(jnp.dot(xn, wg_ref[...], preferred_element_type=F32))
    proj = jnp.dot(p_ref[...].astype(BF16), wp_ref[...], preferred_element_type=F32)
    o_ref[...] = _rmsnorm(h + gate * proj, fnw_ref[...])


def _ple_final(h, p, ple_norm, w_gate, w_proj, final_norm, *, tm):
    t = h.shape[0]
    ple_dim = p.shape[1]
    return pl.pallas_call(
        _ple_final_body,
        grid=(t // tm,),
        in_specs=[
            pl.BlockSpec((tm, D_MODEL), lambda i: (i, 0)),
            pl.BlockSpec((tm, ple_dim), lambda i: (i, 0)),
            pl.BlockSpec((1, D_MODEL), lambda i: (0, 0)),
            pl.BlockSpec((D_MODEL, D_MODEL), lambda i: (0, 0)),
            pl.BlockSpec((ple_dim, D_MODEL), lambda i: (0, 0)),
            pl.BlockSpec((1, D_MODEL), lambda i: (0, 0)),
        ],
        out_specs=pl.BlockSpec((tm, D_MODEL), lambda i: (i, 0)),
        out_shape=jax.ShapeDtypeStruct((t, D_MODEL), F32),
        compiler_params=_params(1),
        name="ple_final",
    )(h, p, ple_norm, w_gate, w_proj, final_norm)


def _head_selector(width):
    src = jnp.arange(LANES, dtype=jnp.int32)[:, None]
    dst = jnp.arange(SSD_HEADS * width, dtype=jnp.int32)[None, :] // width
    return (src == dst).astype(BF16)


def _pad_lanes(v):
    return jnp.pad(v.astype(F32), (0, LANES - v.shape[0]))[None, :]


def kernel(x, p, ffn1_norm, ffn1_w_in, ffn1_w_out, mix_norm, mix_w_in, sc_conv_w, ssd_conv_w,
           ssd_conv_b, ssd_dt_bias, ssd_a_log, ssd_d, ssd_norm, mix_w_out, ffn2_norm, ffn2_w_in,
           ffn2_w_out, ple_norm, ple_w_gate, ple_w_proj, final_norm):
    batch, seq, _ = x.shape
    depth = ffn1_norm.shape[0]
    t = batch * seq
    h = x.reshape(t, D_MODEL)
    e64 = _head_selector(SSD_HEAD_DIM)
    e128 = _head_selector(CHUNK)
    row = lambda v: v.astype(F32)[None, :]

    for i in range(depth):
        h = _ffn(h, row(ffn1_norm[i]), ffn1_w_in[i].astype(BF16), ffn1_w_out[i].astype(BF16),
                 tm=1024, tf=512, name="ffn1")

        w_in = mix_w_in[i]
        w_dt = jnp.pad(w_in[:, MAIN_COLS:], ((0, 0), (0, LANES - SSD_HEADS))).astype(BF16)
        zall, dt_raw = _mix_in(h, row(mix_norm[i]), w_in[:, :MAIN_COLS].astype(BF16), w_dt,
                               tm=1024, tn=1024)
        cw = ssd_conv_w[i].astype(F32)
        cb = ssd_conv_b[i].astype(F32)[None, :]
        ycat = _mixer(zall, dt_raw, sc_conv_w[i].astype(F32),
                      cw[:, :SSD_WIDTH], cw[:, SSD_WIDTH:], cb[:, :SSD_WIDTH], cb[:, SSD_WIDTH:],
                      _pad_lanes(ssd_dt_bias[i]), _pad_lanes(ssd_a_log[i]),
                      jnp.repeat(ssd_d[i].astype(F32), SSD_HEAD_DIM)[None, :],
                      row(ssd_norm[i]), e64, e128, batch=batch, seq=seq)
        h = _out_proj(ycat, mix_w_out[i].astype(BF16), h, tm=1024, tn=512)

        h = _ffn(h, row(ffn2_norm[i]), ffn2_w_in[i].astype(BF16), ffn2_w_out[i].astype(BF16),
                 tm=1024, tf=512, name="ffn2")

        if i + 1 < depth:
            raise NotImplementedError("per-layer embedding step is fused with the final norm")
        h = _ple_final(h, p[i].reshape(t, -1), row(ple_norm[i]), ple_w_gate[i].astype(BF16),
                       ple_w_proj[i].astype(BF16), row(final_norm), tm=512)
    return h.reshape(batch, seq, D_MODEL)
```

```python
import functools

import jax
import jax.numpy as jnp
from jax import lax
from jax.experimental import pallas as pl
from jax.experimental.pallas import tpu as pltpu

F32 = jnp.float32
BF16 = jnp.bfloat16

D_MODEL = 2048
D_FF = 5632
CONV_WIDTH = 2048
SC_KERNEL = 3
SSD_HEADS = 32
SSD_HEAD_DIM = 64
SSD_WIDTH = SSD_HEADS * SSD_HEAD_DIM
SSD_STATE = 128
SSD_GROUPS = 4
SSD_CONV = 4
CHUNK = 128
SSD_BC = SSD_GROUPS * SSD_STATE
SSD_XBC = SSD_WIDTH + 2 * SSD_BC
MIX_WIDTH = CONV_WIDTH + SSD_WIDTH
MAIN_COLS = 3 * CONV_WIDTH + SSD_WIDTH + SSD_XBC
EPS = 1e-6

LANES = 128
SUBLANES = 8
VMEM_LIMIT_BYTES = 60000 * 1024
GROUP_W = SSD_WIDTH // SSD_GROUPS
HEADS_PER_GROUP = SSD_HEADS // SSD_GROUPS
QUAD = 4
QUAD_W = QUAD * SSD_HEAD_DIM


def _params(n_axes):
    return pltpu.CompilerParams(
        dimension_semantics=("arbitrary",) * n_axes,
        vmem_limit_bytes=VMEM_LIMIT_BYTES)


def _rmsnorm(x, w):
    return x * lax.rsqrt(jnp.mean(x * x, axis=-1, keepdims=True) + EPS) * w


def _silu(x):
    return x * jax.nn.sigmoid(x)


def _ffn_body(x_ref, nw_ref, wg_ref, wu_ref, wo_ref, o_ref, xn_ref):
    j = pl.program_id(1)

    @pl.when(j == 0)
    def _():
        xn_ref[...] = _rmsnorm(x_ref[...], nw_ref[...]).astype(BF16)
        o_ref[...] = jnp.zeros_like(o_ref)

    xn = xn_ref[...]
    g = jnp.dot(xn, wg_ref[...], preferred_element_type=F32)
    u = jnp.dot(xn, wu_ref[...], preferred_element_type=F32)
    a = (_silu(g) * u).astype(BF16)
    o_ref[...] += jnp.dot(a, wo_ref[...], preferred_element_type=F32)

    @pl.when(j == pl.num_programs(1) - 1)
    def _():
        o_ref[...] = x_ref[...] + 0.5 * o_ref[...]


def _ffn(h, norm_w, w_in, w_out, *, tm, tf, name):
    t = h.shape[0]
    nj = D_FF // tf
    return pl.pallas_call(
        _ffn_body,
        grid=(t // tm, nj),
        in_specs=[
            pl.BlockSpec((tm, D_MODEL), lambda i, j: (i, 0)),
            pl.BlockSpec((1, D_MODEL), lambda i, j: (0, 0)),
            pl.BlockSpec((D_MODEL, tf), lambda i, j: (0, j)),
            pl.BlockSpec((D_MODEL, tf), lambda i, j: (0, j + nj)),
            pl.BlockSpec((tf, D_MODEL), lambda i, j: (j, 0)),
        ],
        out_specs=pl.BlockSpec((tm, D_MODEL), lambda i, j: (i, 0)),
        out_shape=jax.ShapeDtypeStruct((t, D_MODEL), F32),
        scratch_shapes=[pltpu.VMEM((tm, D_MODEL), BF16)],
        compiler_params=_params(2),
        name=name,
    )(h, norm_w, w_in, w_in, w_out)


def _mix_in_body(x_ref, nw_ref, w_ref, wdt_ref, z_ref, dt_ref, xn_ref):
    j = pl.program_id(1)

    @pl.when(j == 0)
    def _():
        xn = _rmsnorm(x_ref[...], nw_ref[...]).astype(BF16)
        xn_ref[...] = xn
        dt_ref[...] = jnp.dot(xn, wdt_ref[...], preferred_element_type=F32)

    z_ref[...] = jnp.dot(xn_ref[...], w_ref[...].astype(BF16), preferred_element_type=F32)


def _mix_in(h, norm_w, w_main, w_dt, *, tm, tn):
    t = h.shape[0]
    return pl.pallas_call(
        _mix_in_body,
        grid=(t // tm, MAIN_COLS // tn),
        in_specs=[
            pl.BlockSpec((tm, D_MODEL), lambda i, j: (i, 0)),
            pl.BlockSpec((1, D_MODEL), lambda i, j: (0, 0)),
            pl.BlockSpec((D_MODEL, tn), lambda i, j: (0, j)),
            pl.BlockSpec((D_MODEL, LANES), lambda i, j: (0, 0)),
        ],
        out_specs=[
            pl.BlockSpec((tm, tn), lambda i, j: (i, j)),
            pl.BlockSpec((tm, LANES), lambda i, j: (i, 0)),
        ],
        out_shape=[
            jax.ShapeDtypeStruct((t, MAIN_COLS), F32),
            jax.ShapeDtypeStruct((t, LANES), F32),
        ],
        scratch_shapes=[pltpu.VMEM((tm, D_MODEL), BF16)],
        compiler_params=_params(2),
        name="mix_in",
    )(h, norm_w, w_main, w_dt)


N_SPLIT = 3


def _split_cat(a, axis):
    a1 = a.astype(BF16)
    r1 = a - a1.astype(F32)
    a2 = r1.astype(BF16)
    a3 = (r1 - a2.astype(F32)).astype(BF16)
    return jnp.concatenate([a1, a2, a3], axis=axis)


def _causal_conv(buf_ref, w_ref, n_taps):
    ext = buf_ref[...]
    acc = None
    for k in range(n_taps):
        delay = n_taps - 1 - k
        shifted = ext if delay == 0 else pltpu.roll(ext, delay, 0)
        tap = shifted[SUBLANES:, :] * w_ref[k:k + 1, :]
        acc = tap if acc is None else acc + tap
    return acc


def _mixer_body(scb_ref, scc_ref, scx_ref, z_ref, xs_ref, bc_ref, dt_ref,
                scw_ref, cwx_ref, cwbc_ref, cbx_ref, cbbc_ref, dtb_ref, alog_ref,
                dsk_ref, nw_ref, e64_ref, e128_ref,
                y_ref, pbuf, xbuf, bcbuf, st_ref):
    @pl.when(pl.program_id(1) == 0)
    def _():
        pbuf[0:SUBLANES, :] = jnp.zeros((SUBLANES, CONV_WIDTH), F32)
        xbuf[0:SUBLANES, :] = jnp.zeros((SUBLANES, SSD_WIDTH), F32)
        bcbuf[0:SUBLANES, :] = jnp.zeros((SUBLANES, 2 * SSD_BC), F32)
        st_ref[...] = jnp.zeros_like(st_ref)

    body = pl.ds(SUBLANES, CHUNK)
    tail = pl.ds(CHUNK, SUBLANES)

    pbuf[body, :] = scc_ref[...] * scx_ref[...]
    y_a = scb_ref[...] * _causal_conv(pbuf, scw_ref, SC_KERNEL)
    y_ref[:, 0:CONV_WIDTH] = y_a.astype(BF16)
    pbuf[0:SUBLANES, :] = pbuf[tail, :]

    xbuf[body, :] = xs_ref[...]
    bcbuf[body, :] = bc_ref[...]
    xc = _silu(_causal_conv(xbuf, cwx_ref, SSD_CONV) + cbx_ref[...])
    bcc = _silu(_causal_conv(bcbuf, cwbc_ref, SSD_CONV) + cbbc_ref[...])
    xbuf[0:SUBLANES, :] = xbuf[tail, :]
    bcbuf[0:SUBLANES, :] = bcbuf[tail, :]

    dt_in = dt_ref[...] + dtb_ref[...]
    dtv = jnp.maximum(dt_in, 0.0) + jnp.log1p(jnp.exp(-jnp.abs(dt_in)))
    a_neg = -jnp.exp(alog_ref[...])
    row = lax.broadcasted_iota(jnp.int32, (CHUNK, CHUNK), 0)
    col = lax.broadcasted_iota(jnp.int32, (CHUNK, CHUNK), 1)
    causal = col <= row
    tril = causal.astype(BF16)
    acs = jnp.dot(jnp.concatenate([tril] * N_SPLIT, axis=1), _split_cat(dtv * a_neg, 0),
                  preferred_element_type=F32)
    acs_t = acs.T
    decay = jnp.exp(acs[CHUNK - 1:CHUNK, :] - acs)
    exp_acs = jnp.exp(acs)

    per_head = jnp.concatenate([dtv, dtv * decay, exp_acs], axis=0)
    per_chan = jnp.dot(_split_cat(per_head, 1), e64_ref[...], preferred_element_type=F32)
    dt_e = per_chan[0:CHUNK]
    dt_dec_e = per_chan[CHUNK:2 * CHUNK]
    exp_acs_e = per_chan[2 * CHUNK:3 * CHUNK]
    chunk_decay = exp_acs_e[CHUNK - 1:CHUNK, :]
    acs_w = jnp.dot(_split_cat(acs, 1), e128_ref[...], preferred_element_type=F32)

    x_dt_b = (xc * dt_e).astype(BF16)
    x_dec_b = (xc * dt_dec_e).astype(BF16)

    lane_head = lax.broadcasted_iota(jnp.int32, (CHUNK, QUAD_W), 1) // SSD_HEAD_DIM
    y_groups = []
    for g in range(SSD_GROUPS):
        gsl = slice(g * GROUP_W, (g + 1) * GROUP_W)
        b_g = bcc[:, g * SSD_STATE:(g + 1) * SSD_STATE]
        c_g = bcc[:, SSD_BC + g * SSD_STATE:SSD_BC + (g + 1) * SSD_STATE]
        b_gb = b_g.astype(BF16)
        c_gb = c_g.astype(BF16)
        cb = lax.dot_general(c_gb, b_gb, (((1,), (1,)), ((), ())),
                             preferred_element_type=F32)
        cb = jnp.where(causal, cb, 0.0)
        states = jnp.dot(b_g.T.astype(BF16), x_dec_b[:, gsl], preferred_element_type=F32)
        prev = st_ref[:, gsl]
        y_off = jnp.dot(c_gb, prev.astype(BF16), preferred_element_type=F32) * exp_acs_e[:, gsl]
        st_ref[:, gsl] = prev * chunk_decay[:, gsl] + states
        y_diag = []
        for q in range(HEADS_PER_GROUP // QUAD):
            h0 = g * HEADS_PER_GROUP + q * QUAD
            m_parts = []
            x_parts = []
            x_q = x_dt_b[:, h0 * SSD_HEAD_DIM:h0 * SSD_HEAD_DIM + QUAD_W]
            for hh in range(QUAD):
                h = h0 + hh
                diff = acs_w[:, h * CHUNK:(h + 1) * CHUNK] - acs_t[h:h + 1, :]
                m_parts.append((cb * jnp.exp(jnp.where(causal, diff, 0.0))).astype(BF16))
                x_parts.append(jnp.where(lane_head == hh, x_q, jnp.zeros_like(x_q)))
            m_q = jnp.concatenate(m_parts, axis=1)
            x_bd = jnp.concatenate(x_parts, axis=0)
            y_diag.append(jnp.dot(m_q, x_bd, preferred_element_type=F32))
        y_groups.append(jnp.concatenate(y_diag, axis=1) + y_off)
    y = jnp.concatenate(y_groups, axis=1) + dsk_ref[...] * xc

    gz = y * _silu(z_ref[...])
    nw = nw_ref[...]
    for g in range(SSD_GROUPS):
        gsl = slice(g * GROUP_W, (g + 1) * GROUP_W)
        gg = gz[:, gsl]
        yb = gg * lax.rsqrt(jnp.mean(gg * gg, axis=-1, keepdims=True) + EPS) * nw[:, gsl]
        y_ref[:, CONV_WIDTH + g * GROUP_W:CONV_WIDTH + (g + 1) * GROUP_W] = yb.astype(BF16)


def _mixer(zall, dt_raw, sc_w, cw_x, cw_bc, cb_x, cb_bc, dt_bias, a_log, d_skip, norm_w,
           e64, e128, *, batch, seq):
    t = zall.shape[0]
    nc = seq // CHUNK
    bc_blk = (3 * CONV_WIDTH + 2 * SSD_WIDTH) // (2 * SSD_BC)

    def zcol(c):
        return pl.BlockSpec((CHUNK, CONV_WIDTH), lambda b, i, c=c: (b * nc + i, c))

    def full(shape):
        return pl.BlockSpec(shape, lambda b, i: (0,) * len(shape))

    return pl.pallas_call(
        _mixer_body,
        grid=(batch, nc),
        in_specs=[
            zcol(0), zcol(1), zcol(2), zcol(3), zcol(4),
            pl.BlockSpec((CHUNK, 2 * SSD_BC), lambda b, i: (b * nc + i, bc_blk)),
            pl.BlockSpec((CHUNK, LANES), lambda b, i: (b * nc + i, 0)),
            full((SC_KERNEL, CONV_WIDTH)),
            full((SSD_CONV, SSD_WIDTH)),
            full((SSD_CONV, 2 * SSD_BC)),
            full((1, SSD_WIDTH)),
            full((1, 2 * SSD_BC)),
            full((1, LANES)),
            full((1, LANES)),
            full((1, SSD_WIDTH)),
            full((1, SSD_WIDTH)),
            full((N_SPLIT * LANES, SSD_WIDTH)),
            full((N_SPLIT * LANES, SSD_HEADS * CHUNK)),
        ],
        out_specs=pl.BlockSpec((CHUNK, MIX_WIDTH), lambda b, i: (b * nc + i, 0)),
        out_shape=jax.ShapeDtypeStruct((t, MIX_WIDTH), BF16),
        scratch_shapes=[
            pltpu.VMEM((CHUNK + SUBLANES, CONV_WIDTH), F32),
            pltpu.VMEM((CHUNK + SUBLANES, SSD_WIDTH), F32),
            pltpu.VMEM((CHUNK + SUBLANES, 2 * SSD_BC), F32),
            pltpu.VMEM((SSD_STATE, SSD_WIDTH), F32),
        ],
        compiler_params=_params(2),
        name="mixer",
    )(zall, zall, zall, zall, zall, zall, dt_raw,
      sc_w, cw_x, cw_bc, cb_x, cb_bc, dt_bias, a_log, d_skip, norm_w, e64, e128)


def _out_proj_body(y_ref, w_ref, h_ref, o_ref):
    o_ref[...] = h_ref[...] + jnp.dot(y_ref[...], w_ref[...].astype(BF16),
                                      preferred_element_type=F32)


def _out_proj(ycat, w_out, h, *, tm, tn):
    t = h.shape[0]
    return pl.pallas_call(
        _out_proj_body,
        grid=(t // tm, D_MODEL // tn),
        in_specs=[
            pl.BlockSpec((tm, MIX_WIDTH), lambda i, j: (i, 0)),
            pl.BlockSpec((MIX_WIDTH, tn), lambda i, j: (0, j)),
            pl.BlockSpec((tm, tn), lambda i, j: (i, j)),
        ],
        out_specs=pl.BlockSpec((tm, tn), lambda i, j: (i, j)),
        out_shape=jax.ShapeDtypeStruct((t, D_MODEL), F32),
        compiler_params=_params(2),
        name="out_proj",
    )(ycat, w_out, h)


def _ple_final_body(h_ref, p_ref, pnw_ref, wg_ref, wp_ref, fnw_ref, o_ref, *, last_layer):
    h = h_ref[...]
    xn = _rmsnorm(h, pnw_ref[...]).astype(BF16)
    gate = jax.nn.sigmoid(jnp.dot(xn, wg_ref[...], preferred_element_type=F32))
    proj = jnp.dot(p_ref[...].astype(BF16), wp_ref[...], preferred_element_type=F32)
    h = h + gate * proj
    o_ref[...] = _rmsnorm(h, fnw_ref[...]) if last_layer else h


def _ple_final(h, p, ple_norm, w_gate, w_proj, final_norm, *, tm, last_layer):
    t = h.shape[0]
    ple_dim = p.shape[1]
    return pl.pallas_call(
        functools.partial(_ple_final_body, last_layer=last_layer),
        grid=(t // tm,),
        in_specs=[
            pl.BlockSpec((tm, D_MODEL), lambda i: (i, 0)),
            pl.BlockSpec((tm, ple_dim), lambda i: (i, 0)),
            pl.BlockSpec((1, D_MODEL), lambda i: (0, 0)),
            pl.BlockSpec((D_MODEL, D_MODEL), lambda i: (0, 0)),
            pl.BlockSpec((ple_dim, D_MODEL), lambda i: (0, 0)),
            pl.BlockSpec((1, D_MODEL), lambda i: (0, 0)),
        ],
        out_specs=pl.BlockSpec((tm, D_MODEL), lambda i: (i, 0)),
        out_shape=jax.ShapeDtypeStruct((t, D_MODEL), F32),
        compiler_params=_params(1),
        name="ple_final",
    )(h, p, ple_norm, w_gate, w_proj, final_norm)


def _head_selector(width):
    src = jnp.arange(LANES, dtype=jnp.int32)[:, None]
    dst = jnp.arange(SSD_HEADS * width, dtype=jnp.int32)[None, :] // width
    return jnp.tile((src == dst).astype(BF16), (N_SPLIT, 1))


def _pad_lanes(v):
    return jnp.pad(v.astype(F32), (0, LANES - v.shape[0]))[None, :]


def kernel(x, p, ffn1_norm, ffn1_w_in, ffn1_w_out, mix_norm, mix_w_in, sc_conv_w, ssd_conv_w,
           ssd_conv_b, ssd_dt_bias, ssd_a_log, ssd_d, ssd_norm, mix_w_out, ffn2_norm, ffn2_w_in,
           ffn2_w_out, ple_norm, ple_w_gate, ple_w_proj, final_norm):
    batch, seq, _ = x.shape
    depth = ffn1_norm.shape[0]
    t = batch * seq
    h = x.reshape(t, D_MODEL)
    e64 = _head_selector(SSD_HEAD_DIM)
    e128 = _head_selector(CHUNK)
    row = lambda v: v.astype(F32)[None, :]

    for i in range(depth):
        h = _ffn(h, row(ffn1_norm[i]), ffn1_w_in[i].astype(BF16), ffn1_w_out[i].astype(BF16),
                 tm=1024, tf=512, name="ffn1")

        w_in = mix_w_in[i]
        w_dt = jnp.pad(w_in[:, MAIN_COLS:], ((0, 0), (0, LANES - SSD_HEADS))).astype(BF16)
        zall, dt_raw = _mix_in(h, row(mix_norm[i]), w_in, w_dt, tm=1024, tn=1024)
        cw = ssd_conv_w[i].astype(F32)
        cb = ssd_conv_b[i].astype(F32)[None, :]
        ycat = _mixer(zall, dt_raw, sc_conv_w[i].astype(F32),
                      cw[:, :SSD_WIDTH], cw[:, SSD_WIDTH:], cb[:, :SSD_WIDTH], cb[:, SSD_WIDTH:],
                      _pad_lanes(ssd_dt_bias[i]), _pad_lanes(ssd_a_log[i]),
                      jnp.repeat(ssd_d[i].astype(F32), SSD_HEAD_DIM)[None, :],
                      row(ssd_norm[i]), e64, e128, batch=batch, seq=seq)
        h = _out_proj(ycat, mix_w_out[i], h, tm=1024, tn=512)

        h = _ffn(h, row(ffn2_norm[i]), ffn2_w_in[i].astype(BF16), ffn2_w_out[i].astype(BF16),
                 tm=1024, tf=512, name="ffn2")

        h = _ple_final(h, p[i].reshape(t, -1), row(ple_norm[i]), ple_w_gate[i].astype(BF16),
                       ple_w_proj[i].astype(BF16), row(final_norm), tm=512,
                       last_layer=(i + 1 == depth))
    return h.reshape(batch, seq, D_MODEL)
```

```python
import functools

import jax
import jax.numpy as jnp
from jax import lax
from jax.experimental import pallas as pl
from jax.experimental.pallas import tpu as pltpu

F32 = jnp.float32
BF16 = jnp.bfloat16

D_MODEL = 2048
D_FF = 5632
CONV_WIDTH = 2048
SC_KERNEL = 3
SSD_HEADS = 32
SSD_HEAD_DIM = 64
SSD_WIDTH = SSD_HEADS * SSD_HEAD_DIM
SSD_STATE = 128
SSD_GROUPS = 4
SSD_CONV = 4
CHUNK = 128
SSD_BC = SSD_GROUPS * SSD_STATE
SSD_XBC = SSD_WIDTH + 2 * SSD_BC
EPS = 1e-6

LANES = 128
SUBLANES = 8
VMEM_LIMIT_BYTES = 60000 * 1024
GROUP_W = SSD_WIDTH // SSD_GROUPS
HEADS_PER_GROUP = SSD_HEADS // SSD_GROUPS
QUAD = 4
QUAD_W = QUAD * SSD_HEAD_DIM
LOG_SPLIT = 3
SCALE_SPLIT = 2

CONV_IN_COLS = 3 * CONV_WIDTH
SSD_IN_COLS = SSD_WIDTH + SSD_XBC + LANES
Z_OFF, XS_OFF = 0, SSD_WIDTH
BC_OFF = 2 * SSD_WIDTH
DT_OFF = 2 * SSD_WIDTH + 2 * SSD_BC
STEP_ROWS = 2 * CHUNK


def _params(n_axes):
    return pltpu.CompilerParams(
        dimension_semantics=("arbitrary",) * n_axes,
        vmem_limit_bytes=VMEM_LIMIT_BYTES)


def _rmsnorm(x, w):
    return x * lax.rsqrt(jnp.mean(x * x, axis=-1, keepdims=True) + EPS) * w


def _silu(x):
    return x * jax.nn.sigmoid(x)


def _resident(shape):
    return pl.BlockSpec(shape, lambda *_: (0,) * len(shape), pipeline_mode=pl.Buffered(1))


def _ffn_body(x_ref, nw_ref, wg_ref, wu_ref, wo_ref, o_ref, xn_ref):
    j = pl.program_id(1)

    @pl.when(j == 0)
    def _():
        xn_ref[...] = _rmsnorm(x_ref[...], nw_ref[...]).astype(BF16)
        o_ref[...] = jnp.zeros_like(o_ref)

    xn = xn_ref[...]
    g = jnp.dot(xn, wg_ref[...], preferred_element_type=F32)
    u = jnp.dot(xn, wu_ref[...], preferred_element_type=F32)
    a = (_silu(g) * u).astype(BF16)
    o_ref[...] += jnp.dot(a, wo_ref[...], preferred_element_type=F32)

    @pl.when(j == pl.num_programs(1) - 1)
    def _():
        o_ref[...] = x_ref[...] + 0.5 * o_ref[...]


def _ffn(h, norm_w, w_in, w_out, *, tm, tf, name):
    t = h.shape[0]
    nj = D_FF // tf
    return pl.pallas_call(
        _ffn_body,
        grid=(t // tm, nj),
        in_specs=[
            pl.BlockSpec((tm, D_MODEL), lambda i, j: (i, 0)),
            pl.BlockSpec((1, D_MODEL), lambda i, j: (0, 0)),
            pl.BlockSpec((D_MODEL, tf), lambda i, j: (0, j)),
            pl.BlockSpec((D_MODEL, tf), lambda i, j: (0, j + nj)),
            pl.BlockSpec((tf, D_MODEL), lambda i, j: (j, 0)),
        ],
        out_specs=pl.BlockSpec((tm, D_MODEL), lambda i, j: (i, 0)),
        out_shape=jax.ShapeDtypeStruct((t, D_MODEL), F32),
        scratch_shapes=[pltpu.VMEM((tm, D_MODEL), BF16)],
        compiler_params=_params(2),
        name=name,
    )(h, norm_w, w_in, w_in, w_out)


def _causal_conv(buf_ref, w_ref, n_taps):
    ext = buf_ref[...]
    acc = None
    for k in range(n_taps):
        delay = n_taps - 1 - k
        shifted = ext if delay == 0 else pltpu.roll(ext, delay, 0)
        tap = shifted[SUBLANES:, :] * w_ref[k:k + 1, :]
        acc = tap if acc is None else acc + tap
    return acc


_BODY = pl.ds(SUBLANES, CHUNK)
_TAIL = pl.ds(CHUNK, SUBLANES)
_HEAD = pl.ds(0, SUBLANES)


def _in_proj(x, nw_ref, win_ref, z_ref):
    xn = _rmsnorm(x, nw_ref[...]).astype(BF16)
    z_ref[...] = jnp.dot(xn, win_ref[...], preferred_element_type=F32)


def _chunk_pipeline(x_ref, xnext_ref, nw_ref, win_ref, z_a, z_b, mix_chunk):
    @pl.when(pl.program_id(0) == 0)
    def _():
        _in_proj(x_ref[0:CHUNK, :], nw_ref, win_ref, z_a)

    _in_proj(x_ref[CHUNK:STEP_ROWS, :], nw_ref, win_ref, z_b)
    mix_chunk(z_a, pl.ds(0, CHUNK))
    _in_proj(xnext_ref[...], nw_ref, win_ref, z_a)
    mix_chunk(z_b, pl.ds(CHUNK, CHUNK))


def _chunk_specs(t):
    last_chunk = t // CHUNK - 1
    cur = pl.BlockSpec((STEP_ROWS, D_MODEL), lambda s: (s, 0))
    nxt = pl.BlockSpec((CHUNK, D_MODEL), lambda s: (jnp.minimum(2 * s + 2, last_chunk), 0))
    return cur, nxt


def _conv_mix_body(x_ref, xnext_ref, res_ref, nw_ref, win_ref, scw_ref, wout_ref, o_ref,
                   z_a, z_b, pbuf, *, steps_per_seq):
    @pl.when(pl.program_id(0) % steps_per_seq == 0)
    def _():
        pbuf[_HEAD, :] = jnp.zeros((SUBLANES, CONV_WIDTH), F32)

    def mix_chunk(z_ref, rows):
        pbuf[_BODY, :] = z_ref[:, CONV_WIDTH:2 * CONV_WIDTH] * z_ref[:, 2 * CONV_WIDTH:]
        y_a = z_ref[:, 0:CONV_WIDTH] * _causal_conv(pbuf, scw_ref, SC_KERNEL)
        pbuf[_HEAD, :] = pbuf[_TAIL, :]
        o_ref[rows, :] = res_ref[rows, :] + jnp.dot(
            y_a.astype(BF16), wout_ref[...], preferred_element_type=F32)

    _chunk_pipeline(x_ref, xnext_ref, nw_ref, win_ref, z_a, z_b, mix_chunk)


def _conv_mix(h, h_res, norm_w, w_in, sc_w, w_out, *, seq):
    t = h.shape[0]
    cur, nxt = _chunk_specs(t)
    return pl.pallas_call(
        functools.partial(_conv_mix_body, steps_per_seq=seq // STEP_ROWS),
        grid=(t // STEP_ROWS,),
        in_specs=[
            cur, nxt, cur,
            _resident((1, D_MODEL)),
            _resident((D_MODEL, CONV_IN_COLS)),
            _resident((SC_KERNEL, CONV_WIDTH)),
            _resident((CONV_WIDTH, D_MODEL)),
        ],
        out_specs=pl.BlockSpec((STEP_ROWS, D_MODEL), lambda s: (s, 0)),
        out_shape=jax.ShapeDtypeStruct((t, D_MODEL), F32),
        scratch_shapes=[
            pltpu.VMEM((CHUNK, CONV_IN_COLS), F32),
            pltpu.VMEM((CHUNK, CONV_IN_COLS), F32),
            pltpu.VMEM((CHUNK + SUBLANES, CONV_WIDTH), F32),
        ],
        compiler_params=_params(1),
        name="conv_mix",
    )(h, h, h_res, norm_w, w_in, sc_w, w_out)


def _split_cat(a, axis, n_terms):
    terms = []
    rest = a
    for _ in range(n_terms):
        term = rest.astype(BF16)
        terms.append(term)
        rest = rest - term.astype(F32)
    return jnp.concatenate(terms, axis=axis)


def _ssd_chunk(z_ref, cwx_ref, cwbc_ref, cbx_ref, cbbc_ref, dtb_ref, alog_ref, dsk_ref,
               gnw_ref, e64_ref, e128_ref, xbuf, bcbuf, st_ref):
    xbuf[_BODY, :] = z_ref[:, XS_OFF:XS_OFF + SSD_WIDTH]
    bcbuf[_BODY, :] = z_ref[:, BC_OFF:BC_OFF + 2 * SSD_BC]
    xc = _silu(_causal_conv(xbuf, cwx_ref, SSD_CONV) + cbx_ref[...])
    bcc = _silu(_causal_conv(bcbuf, cwbc_ref, SSD_CONV) + cbbc_ref[...])
    xbuf[_HEAD, :] = xbuf[_TAIL, :]
    bcbuf[_HEAD, :] = bcbuf[_TAIL, :]

    dt_in = z_ref[:, DT_OFF:DT_OFF + LANES] + dtb_ref[...]
    dtv = jnp.maximum(dt_in, 0.0) + jnp.log1p(jnp.exp(-jnp.abs(dt_in)))
    a_neg = -jnp.exp(alog_ref[...])
    row = lax.broadcasted_iota(jnp.int32, (CHUNK, CHUNK), 0)
    col = lax.broadcasted_iota(jnp.int32, (CHUNK, CHUNK), 1)
    causal = col <= row
    tril = causal.astype(BF16)
    acs = jnp.dot(jnp.concatenate([tril] * LOG_SPLIT, axis=1),
                  _split_cat(dtv * a_neg, 0, LOG_SPLIT),
                  preferred_element_type=F32)
    acs_t = acs.T
    decay = jnp.exp(acs[CHUNK - 1:CHUNK, :] - acs)
    exp_acs = jnp.exp(acs)

    per_head = jnp.concatenate([dtv, dtv * decay, exp_acs], axis=0)
    per_chan = jnp.dot(_split_cat(per_head, 1, SCALE_SPLIT), e64_ref[...],
                       preferred_element_type=F32)
    dt_e = per_chan[0:CHUNK]
    dt_dec_e = per_chan[CHUNK:2 * CHUNK]
    exp_acs_e = per_chan[2 * CHUNK:3 * CHUNK]
    chunk_decay = exp_acs_e[CHUNK - 1:CHUNK, :]
    acs_w = jnp.dot(_split_cat(acs, 1, LOG_SPLIT), e128_ref[...],
                    preferred_element_type=F32)

    x_dt_b = (xc * dt_e).astype(BF16)
    x_dec_b = (xc * dt_dec_e).astype(BF16)

    lane_head = lax.broadcasted_iota(jnp.int32, (CHUNK, QUAD_W), 1) // SSD_HEAD_DIM
    y_groups = []
    for g in range(SSD_GROUPS):
        gsl = slice(g * GROUP_W, (g + 1) * GROUP_W)
        b_g = bcc[:, g * SSD_STATE:(g + 1) * SSD_STATE]
        c_g = bcc[:, SSD_BC + g * SSD_STATE:SSD_BC + (g + 1) * SSD_STATE]
        b_gb = b_g.astype(BF16)
        c_gb = c_g.astype(BF16)
        cb = lax.dot_general(c_gb, b_gb, (((1,), (1,)), ((), ())),
                             preferred_element_type=F32)
        cb = jnp.where(causal, cb, 0.0)
        states = jnp.dot(b_g.T.astype(BF16), x_dec_b[:, gsl], preferred_element_type=F32)
        prev = st_ref[:, gsl]
        y_off = jnp.dot(c_gb, prev.astype(BF16), preferred_element_type=F32) * exp_acs_e[:, gsl]
        st_ref[:, gsl] = prev * chunk_decay[:, gsl] + states
        y_diag = []
        for q in range(HEADS_PER_GROUP // QUAD):
            h0 = g * HEADS_PER_GROUP + q * QUAD
            m_parts = []
            x_parts = []
            x_q = x_dt_b[:, h0 * SSD_HEAD_DIM:h0 * SSD_HEAD_DIM + QUAD_W]
            for hh in range(QUAD):
                h = h0 + hh
                diff = acs_w[:, h * CHUNK:(h + 1) * CHUNK] - acs_t[h:h + 1, :]
                m_parts.append((cb * jnp.exp(jnp.where(causal, diff, 0.0))).astype(BF16))
                x_parts.append(jnp.where(lane_head == hh, x_q, jnp.zeros_like(x_q)))
            m_q = jnp.concatenate(m_parts, axis=1)
            x_bd = jnp.concatenate(x_parts, axis=0)
            y_diag.append(jnp.dot(m_q, x_bd, preferred_element_type=F32))
        y_groups.append(jnp.concatenate(y_diag, axis=1) + y_off)
    y = jnp.concatenate(y_groups, axis=1) + dsk_ref[...] * xc

    gz = y * _silu(z_ref[:, Z_OFF:Z_OFF + SSD_WIDTH])
    gnw = gnw_ref[...]
    out = []
    for g in range(SSD_GROUPS):
        gsl = slice(g * GROUP_W, (g + 1) * GROUP_W)
        gg = gz[:, gsl]
        out.append(gg * lax.rsqrt(jnp.mean(gg * gg, axis=-1, keepdims=True) + EPS) * gnw[:, gsl])
    return jnp.concatenate(out, axis=1)


def _ssd_mix_body(x_ref, xnext_ref, nw_ref, win_ref, cwx_ref, cwbc_ref, cbx_ref, cbbc_ref,
                  dtb_ref, alog_ref, dsk_ref, gnw_ref, e64_ref, e128_ref, wout_ref, o_ref,
                  z_a, z_b, xbuf, bcbuf, st_ref, *, steps_per_seq):
    @pl.when(pl.program_id(0) % steps_per_seq == 0)
    def _():
        xbuf[_HEAD, :] = jnp.zeros((SUBLANES, SSD_WIDTH), F32)
        bcbuf[_HEAD, :] = jnp.zeros((SUBLANES, 2 * SSD_BC), F32)
        st_ref[...] = jnp.zeros_like(st_ref)

    def mix_chunk(z_ref, rows):
        y_b = _ssd_chunk(z_ref, cwx_ref, cwbc_ref, cbx_ref, cbbc_ref, dtb_ref, alog_ref,
                         dsk_ref, gnw_ref, e64_ref, e128_ref, xbuf, bcbuf, st_ref)
        o_ref[rows, :] = x_ref[rows, :] + jnp.dot(
            y_b.astype(BF16), wout_ref[...], preferred_element_type=F32)

    _chunk_pipeline(x_ref, xnext_ref, nw_ref, win_ref, z_a, z_b, mix_chunk)


def _ssd_mix(h, norm_w, w_in, cw_x, cw_bc, cb_x, cb_bc, dt_bias, a_log, d_skip,
             gnorm_w, e64, e128, w_out, *, seq):
    t = h.shape[0]
    cur, nxt = _chunk_specs(t)
    return pl.pallas_call(
        functools.partial(_ssd_mix_body, steps_per_seq=seq // STEP_ROWS),
        grid=(t // STEP_ROWS,),
        in_specs=[
            cur, nxt,
            _resident((1, D_MODEL)),
            _resident((D_MODEL, SSD_IN_COLS)),
            _resident((SSD_CONV, SSD_WIDTH)),
            _resident((SSD_CONV, 2 * SSD_BC)),
            _resident((1, SSD_WIDTH)),
            _resident((1, 2 * SSD_BC)),
            _resident((1, LANES)),
            _resident((1, LANES)),
            _resident((1, SSD_WIDTH)),
            _resident((1, SSD_WIDTH)),
            _resident((SCALE_SPLIT * LANES, SSD_WIDTH)),
            _resident((LOG_SPLIT * LANES, SSD_HEADS * CHUNK)),
            _resident((SSD_WIDTH, D_MODEL)),
        ],
        out_specs=pl.BlockSpec((STEP_ROWS, D_MODEL), lambda s: (s, 0)),
        out_shape=jax.ShapeDtypeStruct((t, D_MODEL), F32),
        scratch_shapes=[
            pltpu.VMEM((CHUNK, SSD_IN_COLS), F32),
            pltpu.VMEM((CHUNK, SSD_IN_COLS), F32),
            pltpu.VMEM((CHUNK + SUBLANES, SSD_WIDTH), F32),
            pltpu.VMEM((CHUNK + SUBLANES, 2 * SSD_BC), F32),
            pltpu.VMEM((SSD_STATE, SSD_WIDTH), F32),
        ],
        compiler_params=_params(1),
        name="ssd_mix",
    )(h, h, norm_w, w_in, cw_x, cw_bc, cb_x, cb_bc, dt_bias, a_log,
      d_skip, gnorm_w, e64, e128, w_out)


def _ple_final_body(h_ref, p_ref, pnw_ref, wg_ref, wp_ref, fnw_ref, o_ref, *, last_layer):
    h = h_ref[...]
    xn = _rmsnorm(h, pnw_ref[...]).astype(BF16)
    gate = jax.nn.sigmoid(jnp.dot(xn, wg_ref[...], preferred_element_type=F32))
    proj = jnp.dot(p_ref[...].astype(BF16), wp_ref[...], preferred_element_type=F32)
    h = h + gate * proj
    o_ref[...] = _rmsnorm(h, fnw_ref[...]) if last_layer else h


def _ple_final(h, p, ple_norm, w_gate, w_proj, final_norm, *, tm, last_layer):
    t = h.shape[0]
    ple_dim = p.shape[1]
    return pl.pallas_call(
        functools.partial(_ple_final_body, last_layer=last_layer),
        grid=(t // tm,),
        in_specs=[
            pl.BlockSpec((tm, D_MODEL), lambda i: (i, 0)),
            pl.BlockSpec((tm, ple_dim), lambda i: (i, 0)),
            pl.BlockSpec((1, D_MODEL), lambda i: (0, 0)),
            pl.BlockSpec((D_MODEL, D_MODEL), lambda i: (0, 0)),
            pl.BlockSpec((ple_dim, D_MODEL), lambda i: (0, 0)),
            pl.BlockSpec((1, D_MODEL), lambda i: (0, 0)),
        ],
        out_specs=pl.BlockSpec((tm, D_MODEL), lambda i: (i, 0)),
        out_shape=jax.ShapeDtypeStruct((t, D_MODEL), F32),
        compiler_params=_params(1),
        name="ple_final",
    )(h, p, ple_norm, w_gate, w_proj, final_norm)


def _head_selector(width, n_terms):
    src = jnp.arange(LANES, dtype=jnp.int32)[:, None]
    dst = jnp.arange(SSD_HEADS * width, dtype=jnp.int32)[None, :] // width
    return jnp.tile((src == dst).astype(BF16), (n_terms, 1))


def _pad_lanes(v):
    return jnp.pad(v.astype(F32), (0, LANES - v.shape[0]))[None, :]


def kernel(x, p, ffn1_norm, ffn1_w_in, ffn1_w_out, mix_norm, mix_w_in, sc_conv_w, ssd_conv_w,
           ssd_conv_b, ssd_dt_bias, ssd_a_log, ssd_d, ssd_norm, mix_w_out, ffn2_norm, ffn2_w_in,
           ffn2_w_out, ple_norm, ple_w_gate, ple_w_proj, final_norm):
    batch, seq, _ = x.shape
    depth = ffn1_norm.shape[0]
    t = batch * seq
    h = x.reshape(t, D_MODEL)
    e64 = _head_selector(SSD_HEAD_DIM, SCALE_SPLIT)
    e128 = _head_selector(CHUNK, LOG_SPLIT)
    row = lambda v: v.astype(F32)[None, :]

    for i in range(depth):
        h = _ffn(h, row(ffn1_norm[i]), ffn1_w_in[i].astype(BF16), ffn1_w_out[i].astype(BF16),
                 tm=1024, tf=512, name="ffn1")

        w_in = mix_w_in[i]
        w_in_conv = w_in[:, :CONV_IN_COLS].astype(BF16)
        w_in_ssd = jnp.pad(w_in[:, CONV_IN_COLS:],
                           ((0, 0), (0, LANES - SSD_HEADS))).astype(BF16)
        w_out = mix_w_out[i]
        cw = ssd_conv_w[i].astype(F32)
        cb = ssd_conv_b[i].astype(F32)[None, :]
        h_ssd = _ssd_mix(h, row(mix_norm[i]), w_in_ssd,
                         cw[:, :SSD_WIDTH], cw[:, SSD_WIDTH:], cb[:, :SSD_WIDTH], cb[:, SSD_WIDTH:],
                         _pad_lanes(ssd_dt_bias[i]), _pad_lanes(ssd_a_log[i]),
                         jnp.repeat(ssd_d[i].astype(F32), SSD_HEAD_DIM)[None, :],
                         row(ssd_norm[i]), e64, e128, w_out[CONV_WIDTH:].astype(BF16), seq=seq)
        h = _conv_mix(h, h_ssd, row(mix_norm[i]), w_in_conv, sc_conv_w[i].astype(F32),
                      w_out[:CONV_WIDTH].astype(BF16), seq=seq)

        h = _ffn(h, row(ffn2_norm[i]), ffn2_w_in[i].astype(BF16), ffn2_w_out[i].astype(BF16),
                 tm=1024, tf=512, name="ffn2")

        h = _ple_final(h, p[i].reshape(t, -1), row(ple_norm[i]), ple_w_gate[i].astype(BF16),
                       ple_w_proj[i].astype(BF16), row(final_norm), tm=512,
                       last_layer=(i + 1 == depth))
    return h.reshape(batch, seq, D_MODEL)
```

```python
import functools

import jax
import jax.numpy as jnp
from jax import lax
from jax.experimental import pallas as pl
from jax.experimental.pallas import tpu as pltpu

F32 = jnp.float32
BF16 = jnp.bfloat16

D_MODEL = 2048
D_FF = 5632
CONV_WIDTH = 2048
SC_KERNEL = 3
SSD_HEADS = 32
SSD_HEAD_DIM = 64
SSD_WIDTH = SSD_HEADS * SSD_HEAD_DIM
SSD_STATE = 128
SSD_GROUPS = 4
SSD_CONV = 4
CHUNK = 128
SSD_BC = SSD_GROUPS * SSD_STATE
SSD_XBC = SSD_WIDTH + 2 * SSD_BC
EPS = 1e-6

LANES = 128
SUBLANES = 8
VMEM_LIMIT_BYTES = 60000 * 1024
GROUP_W = SSD_WIDTH // SSD_GROUPS
HEADS_PER_GROUP = SSD_HEADS // SSD_GROUPS
QUAD = 4
QUAD_W = QUAD * SSD_HEAD_DIM
LOG_SPLIT = 3
SCALE_SPLIT = 2

CONV_IN_COLS = 3 * CONV_WIDTH
MAIN_COLS = CONV_IN_COLS + SSD_WIDTH + SSD_XBC
SSD_IN_COLS = SSD_WIDTH + SSD_XBC + LANES
Z_OFF, XS_OFF = 0, SSD_WIDTH
BC_OFF = 2 * SSD_WIDTH
DT_OFF = 2 * SSD_WIDTH + 2 * SSD_BC
STEP_ROWS = 2 * CHUNK


def _params(n_axes):
    return pltpu.CompilerParams(
        dimension_semantics=("arbitrary",) * n_axes,
        vmem_limit_bytes=VMEM_LIMIT_BYTES)


def _rmsnorm(x, w):
    return x * lax.rsqrt(jnp.mean(x * x, axis=-1, keepdims=True) + EPS) * w


def _silu(x):
    return x * jax.nn.sigmoid(x)


def _resident(shape, block_index=None):
    index = (0,) * len(shape) if block_index is None else block_index
    return pl.BlockSpec(shape, lambda *_: index, pipeline_mode=pl.Buffered(1))


def _ffn_body(x_ref, nw_ref, wg_ref, wu_ref, wo_ref, o_ref, xn_ref):
    j = pl.program_id(1)

    @pl.when(j == 0)
    def _():
        xn_ref[...] = _rmsnorm(x_ref[...], nw_ref[...]).astype(BF16)
        o_ref[...] = jnp.zeros_like(o_ref)

    xn = xn_ref[...]
    g = jnp.dot(xn, wg_ref[...], preferred_element_type=F32)
    u = jnp.dot(xn, wu_ref[...], preferred_element_type=F32)
    a = (_silu(g) * u).astype(BF16)
    o_ref[...] += jnp.dot(a, wo_ref[...], preferred_element_type=F32)

    @pl.when(j == pl.num_programs(1) - 1)
    def _():
        o_ref[...] = x_ref[...] + 0.5 * o_ref[...]


def _ffn(h, norm_w, w_in, w_out, *, tm, tf, name):
    t = h.shape[0]
    nj = D_FF // tf
    return pl.pallas_call(
        _ffn_body,
        grid=(t // tm, nj),
        in_specs=[
            pl.BlockSpec((tm, D_MODEL), lambda i, j: (i, 0)),
            pl.BlockSpec((1, D_MODEL), lambda i, j: (0, 0)),
            pl.BlockSpec((D_MODEL, tf), lambda i, j: (0, j)),
            pl.BlockSpec((D_MODEL, tf), lambda i, j: (0, j + nj)),
            pl.BlockSpec((tf, D_MODEL), lambda i, j: (j, 0)),
        ],
        out_specs=pl.BlockSpec((tm, D_MODEL), lambda i, j: (i, 0)),
        out_shape=jax.ShapeDtypeStruct((t, D_MODEL), F32),
        scratch_shapes=[pltpu.VMEM((tm, D_MODEL), BF16)],
        compiler_params=_params(2),
        name=name,
    )(h, norm_w, w_in, w_in, w_out)


def _causal_conv(buf_ref, w_ref, n_taps):
    ext = buf_ref[...]
    acc = None
    for k in range(n_taps):
        delay = n_taps - 1 - k
        shifted = ext if delay == 0 else pltpu.roll(ext, delay, 0)
        tap = shifted[SUBLANES:, :] * w_ref[k:k + 1, :]
        acc = tap if acc is None else acc + tap
    return acc


_BODY = pl.ds(SUBLANES, CHUNK)
_TAIL = pl.ds(CHUNK, SUBLANES)
_HEAD = pl.ds(0, SUBLANES)


def _in_proj(x, nw_ref, win_refs, z_ref):
    xn = _rmsnorm(x, nw_ref[...]).astype(BF16)
    col = 0
    for w_ref in win_refs:
        width = w_ref.shape[1]
        z_ref[:, col:col + width] = jnp.dot(xn, w_ref[...], preferred_element_type=F32)
        col += width


def _chunk_pipeline(x_ref, xnext_ref, nw_ref, win_refs, z_a, z_b, mix_chunk):
    @pl.when(pl.program_id(0) == 0)
    def _():
        _in_proj(x_ref[0:CHUNK, :], nw_ref, win_refs, z_a)

    _in_proj(x_ref[CHUNK:STEP_ROWS, :], nw_ref, win_refs, z_b)
    mix_chunk(z_a, pl.ds(0, CHUNK))
    _in_proj(xnext_ref[...], nw_ref, win_refs, z_a)
    mix_chunk(z_b, pl.ds(CHUNK, CHUNK))


def _chunk_specs(t):
    last_chunk = t // CHUNK - 1
    cur = pl.BlockSpec((STEP_ROWS, D_MODEL), lambda s: (s, 0))
    nxt = pl.BlockSpec((CHUNK, D_MODEL), lambda s: (jnp.minimum(2 * s + 2, last_chunk), 0))
    return cur, nxt


def _conv_mix_body(x_ref, xnext_ref, res_ref, nw_ref, win_ref, scw_ref, wout_ref, o_ref,
                   z_a, z_b, pbuf, *, steps_per_seq):
    @pl.when(pl.program_id(0) % steps_per_seq == 0)
    def _():
        pbuf[_HEAD, :] = jnp.zeros((SUBLANES, CONV_WIDTH), F32)

    def mix_chunk(z_ref, rows):
        pbuf[_BODY, :] = z_ref[:, CONV_WIDTH:2 * CONV_WIDTH] * z_ref[:, 2 * CONV_WIDTH:]
        y_a = z_ref[:, 0:CONV_WIDTH] * _causal_conv(pbuf, scw_ref, SC_KERNEL)
        pbuf[_HEAD, :] = pbuf[_TAIL, :]
        o_ref[rows, :] = res_ref[rows, :] + jnp.dot(
            y_a.astype(BF16), wout_ref[...], preferred_element_type=F32)

    _chunk_pipeline(x_ref, xnext_ref, nw_ref, [win_ref], z_a, z_b, mix_chunk)


def _conv_mix(h, h_res, norm_w, w_in, sc_w, w_out, *, seq):
    t = h.shape[0]
    cur, nxt = _chunk_specs(t)
    return pl.pallas_call(
        functools.partial(_conv_mix_body, steps_per_seq=seq // STEP_ROWS),
        grid=(t // STEP_ROWS,),
        in_specs=[
            cur, nxt, cur,
            _resident((1, D_MODEL)),
            _resident((D_MODEL, CONV_IN_COLS), (0, 0)),
            _resident((SC_KERNEL, CONV_WIDTH)),
            _resident((CONV_WIDTH, D_MODEL), (0, 0)),
        ],
        out_specs=pl.BlockSpec((STEP_ROWS, D_MODEL), lambda s: (s, 0)),
        out_shape=jax.ShapeDtypeStruct((t, D_MODEL), F32),
        scratch_shapes=[
            pltpu.VMEM((CHUNK, CONV_IN_COLS), F32),
            pltpu.VMEM((CHUNK, CONV_IN_COLS), F32),
            pltpu.VMEM((CHUNK + SUBLANES, CONV_WIDTH), F32),
        ],
        compiler_params=_params(1),
        name="conv_mix",
    )(h, h, h_res, norm_w, w_in, sc_w, w_out)


def _split_cat(a, axis, n_terms):
    terms = []
    rest = a
    for _ in range(n_terms):
        term = rest.astype(BF16)
        terms.append(term)
        rest = rest - term.astype(F32)
    return jnp.concatenate(terms, axis=axis)


def _ssd_chunk(z_ref, cwx_ref, cwbc_ref, cbx_ref, cbbc_ref, dtb_ref, alog_ref, dsk_ref,
               gnw_ref, e64_ref, xbuf, bcbuf, st_ref):
    xbuf[_BODY, :] = z_ref[:, XS_OFF:XS_OFF + SSD_WIDTH]
    bcbuf[_BODY, :] = z_ref[:, BC_OFF:BC_OFF + 2 * SSD_BC]
    xc = _silu(_causal_conv(xbuf, cwx_ref, SSD_CONV) + cbx_ref[...])
    bcc = _silu(_causal_conv(bcbuf, cwbc_ref, SSD_CONV) + cbbc_ref[...])
    xbuf[_HEAD, :] = xbuf[_TAIL, :]
    bcbuf[_HEAD, :] = bcbuf[_TAIL, :]

    dt_in = z_ref[:, DT_OFF:DT_OFF + LANES] + dtb_ref[...]
    dtv = jnp.maximum(dt_in, 0.0) + jnp.log1p(jnp.exp(-jnp.abs(dt_in)))
    a_neg = -jnp.exp(alog_ref[...])
    row = lax.broadcasted_iota(jnp.int32, (CHUNK, CHUNK), 0)
    col = lax.broadcasted_iota(jnp.int32, (CHUNK, CHUNK), 1)
    causal = col <= row
    tril = causal.astype(BF16)
    acs = jnp.dot(jnp.concatenate([tril] * LOG_SPLIT, axis=1),
                  _split_cat(dtv * a_neg, 0, LOG_SPLIT),
                  preferred_element_type=F32)
    acs_t = acs.T
    decay = jnp.exp(acs[CHUNK - 1:CHUNK, :] - acs)
    exp_acs = jnp.exp(acs)

    per_head = jnp.concatenate([dtv, dtv * decay, exp_acs], axis=0)
    per_chan = jnp.dot(_split_cat(per_head, 1, SCALE_SPLIT), e64_ref[...],
                       preferred_element_type=F32)
    dt_e = per_chan[0:CHUNK]
    dt_dec_e = per_chan[CHUNK:2 * CHUNK]
    exp_acs_e = per_chan[2 * CHUNK:3 * CHUNK]
    chunk_decay = exp_acs_e[CHUNK - 1:CHUNK, :]

    x_dt_b = (xc * dt_e).astype(BF16)
    x_dec_b = (xc * dt_dec_e).astype(BF16)

    lane_head = lax.broadcasted_iota(jnp.int32, (CHUNK, QUAD_W), 1) // SSD_HEAD_DIM
    y_groups = []
    for g in range(SSD_GROUPS):
        gsl = slice(g * GROUP_W, (g + 1) * GROUP_W)
        b_g = bcc[:, g * SSD_STATE:(g + 1) * SSD_STATE]
        c_g = bcc[:, SSD_BC + g * SSD_STATE:SSD_BC + (g + 1) * SSD_STATE]
        b_gb = b_g.astype(BF16)
        c_gb = c_g.astype(BF16)
        cb = lax.dot_general(c_gb, b_gb, (((1,), (1,)), ((), ())),
                             preferred_element_type=F32)
        cb = jnp.where(causal, cb, 0.0)
        states = jnp.dot(b_g.T.astype(BF16), x_dec_b[:, gsl], preferred_element_type=F32)
        prev = st_ref[:, gsl]
        y_off = jnp.dot(c_gb, prev.astype(BF16), preferred_element_type=F32) * exp_acs_e[:, gsl]
        st_ref[:, gsl] = prev * chunk_decay[:, gsl] + states
        y_diag = []
        for q in range(HEADS_PER_GROUP // QUAD):
            h0 = g * HEADS_PER_GROUP + q * QUAD
            m_parts = []
            x_parts = []
            x_q = x_dt_b[:, h0 * SSD_HEAD_DIM:h0 * SSD_HEAD_DIM + QUAD_W]
            for hh in range(QUAD):
                h = h0 + hh
                diff = jnp.broadcast_to(acs[:, h:h + 1], (CHUNK, CHUNK)) - acs_t[h:h + 1, :]
                m_parts.append((cb * jnp.exp(jnp.where(causal, diff, 0.0))).astype(BF16))
                x_parts.append(jnp.where(lane_head == hh, x_q, jnp.zeros_like(x_q)))
            m_q = jnp.concatenate(m_parts, axis=1)
            x_bd = jnp.concatenate(x_parts, axis=0)
            y_diag.append(jnp.dot(m_q, x_bd, preferred_element_type=F32))
        y_groups.append(jnp.concatenate(y_diag, axis=1) + y_off)
    y = jnp.concatenate(y_groups, axis=1) + dsk_ref[...] * xc

    gz = y * _silu(z_ref[:, Z_OFF:Z_OFF + SSD_WIDTH])
    gnw = gnw_ref[...]
    out = []
    for g in range(SSD_GROUPS):
        gsl = slice(g * GROUP_W, (g + 1) * GROUP_W)
        gg = gz[:, gsl]
        out.append(gg * lax.rsqrt(jnp.mean(gg * gg, axis=-1, keepdims=True) + EPS) * gnw[:, gsl])
    return jnp.concatenate(out, axis=1)


def _ssd_mix_body(x_ref, xnext_ref, nw_ref, wz_ref, wx0_ref, wx1_ref, wbc_ref, wdt_ref,
                  cwx_ref, cwbc_ref, cbx_ref, cbbc_ref,
                  dtb_ref, alog_ref, dsk_ref, gnw_ref, e64_ref, wout_ref, o_ref,
                  z_a, z_b, xbuf, bcbuf, st_ref, *, steps_per_seq):
    @pl.when(pl.program_id(0) % steps_per_seq == 0)
    def _():
        xbuf[_HEAD, :] = jnp.zeros((SUBLANES, SSD_WIDTH), F32)
        bcbuf[_HEAD, :] = jnp.zeros((SUBLANES, 2 * SSD_BC), F32)
        st_ref[...] = jnp.zeros_like(st_ref)

    def mix_chunk(z_ref, rows):
        y_b = _ssd_chunk(z_ref, cwx_ref, cwbc_ref, cbx_ref, cbbc_ref, dtb_ref, alog_ref,
                         dsk_ref, gnw_ref, e64_ref, xbuf, bcbuf, st_ref)
        o_ref[rows, :] = x_ref[rows, :] + jnp.dot(
            y_b.astype(BF16), wout_ref[...], preferred_element_type=F32)

    win_refs = [wz_ref, wx0_ref, wx1_ref, wbc_ref, wdt_ref]
    _chunk_pipeline(x_ref, xnext_ref, nw_ref, win_refs, z_a, z_b, mix_chunk)


def _ssd_mix(h, norm_w, w_in, w_dt, cw_x, cw_bc, cb_x, cb_bc, dt_bias, a_log, d_skip,
             gnorm_w, e64, w_out, *, seq):
    t = h.shape[0]
    half = SSD_WIDTH // 2
    cur, nxt = _chunk_specs(t)
    return pl.pallas_call(
        functools.partial(_ssd_mix_body, steps_per_seq=seq // STEP_ROWS),
        grid=(t // STEP_ROWS,),
        in_specs=[
            cur, nxt,
            _resident((1, D_MODEL)),
            _resident((D_MODEL, SSD_WIDTH), (0, CONV_IN_COLS // SSD_WIDTH)),
            _resident((D_MODEL, half), (0, (CONV_IN_COLS + SSD_WIDTH) // half)),
            _resident((D_MODEL, half), (0, (CONV_IN_COLS + SSD_WIDTH) // half + 1)),
            _resident((D_MODEL, 2 * SSD_BC), (0, (CONV_IN_COLS + 2 * SSD_WIDTH) // (2 * SSD_BC))),
            _resident((D_MODEL, LANES)),
            _resident((SSD_CONV, SSD_WIDTH)),
            _resident((SSD_CONV, 2 * SSD_BC)),
            _resident((1, SSD_WIDTH)),
            _resident((1, 2 * SSD_BC)),
            _resident((1, LANES)),
            _resident((1, LANES)),
            _resident((1, SSD_WIDTH)),
            _resident((1, SSD_WIDTH)),
            _resident((SCALE_SPLIT * LANES, SSD_WIDTH)),
            _resident((SSD_WIDTH, D_MODEL), (1, 0)),
        ],
        out_specs=pl.BlockSpec((STEP_ROWS, D_MODEL), lambda s: (s, 0)),
        out_shape=jax.ShapeDtypeStruct((t, D_MODEL), F32),
        scratch_shapes=[
            pltpu.VMEM((CHUNK, SSD_IN_COLS), F32),
            pltpu.VMEM((CHUNK, SSD_IN_COLS), F32),
            pltpu.VMEM((CHUNK + SUBLANES, SSD_WIDTH), F32),
            pltpu.VMEM((CHUNK + SUBLANES, 2 * SSD_BC), F32),
            pltpu.VMEM((SSD_STATE, SSD_WIDTH), F32),
        ],
        compiler_params=_params(1),
        name="ssd_mix",
    )(h, h, norm_w, w_in, w_in, w_in, w_in, w_dt, cw_x, cw_bc, cb_x, cb_bc, dt_bias, a_log,
      d_skip, gnorm_w, e64, w_out)


def _transpose_cast_body(w_ref, o_ref):
    o_ref[...] = w_ref[...].T.astype(BF16)


def _transpose_cast(w_t, *, n_rows, bn):
    d = w_t.shape[1]
    return pl.pallas_call(
        _transpose_cast_body,
        grid=(n_rows // bn,),
        in_specs=[pl.BlockSpec((bn, d), lambda j: (j, 0))],
        out_specs=pl.BlockSpec((d, bn), lambda j: (0, j)),
        out_shape=jax.ShapeDtypeStruct((d, n_rows), BF16),
        compiler_params=_params(1),
        name="transpose_cast",
    )(w_t)


def _ple_final_body(h_ref, p_ref, pnw_ref, wg_ref, wp_ref, fnw_ref, o_ref, *, last_layer):
    h = h_ref[...]
    xn = _rmsnorm(h, pnw_ref[...]).astype(BF16)
    gate = jax.nn.sigmoid(jnp.dot(xn, wg_ref[...], preferred_element_type=F32))
    proj = jnp.dot(p_ref[...].astype(BF16), wp_ref[...], preferred_element_type=F32)
    h = h + gate * proj
    o_ref[...] = _rmsnorm(h, fnw_ref[...]) if last_layer else h


def _ple_final(h, p, ple_norm, w_gate, w_proj, final_norm, *, tm, last_layer):
    t = h.shape[0]
    ple_dim = p.shape[1]
    return pl.pallas_call(
        functools.partial(_ple_final_body, last_layer=last_layer),
        grid=(t // tm,),
        in_specs=[
            pl.BlockSpec((tm, D_MODEL), lambda i: (i, 0)),
            pl.BlockSpec((tm, ple_dim), lambda i: (i, 0)),
            pl.BlockSpec((1, D_MODEL), lambda i: (0, 0)),
            pl.BlockSpec((D_MODEL, D_MODEL), lambda i: (0, 0)),
            pl.BlockSpec((ple_dim, D_MODEL), lambda i: (0, 0)),
            pl.BlockSpec((1, D_MODEL), lambda i: (0, 0)),
        ],
        out_specs=pl.BlockSpec((tm, D_MODEL), lambda i: (i, 0)),
        out_shape=jax.ShapeDtypeStruct((t, D_MODEL), F32),
        compiler_params=_params(1),
        name="ple_final",
    )(h, p, ple_norm, w_gate, w_proj, final_norm)


def _head_selector(width, n_terms):
    src = jnp.arange(LANES, dtype=jnp.int32)[:, None]
    dst = jnp.arange(SSD_HEADS * width, dtype=jnp.int32)[None, :] // width
    return jnp.tile((src == dst).astype(BF16), (n_terms, 1))


def _pad_lanes(v):
    return jnp.pad(v.astype(F32), (0, LANES - v.shape[0]))[None, :]


def kernel(x, p, ffn1_norm, ffn1_w_in, ffn1_w_out, mix_norm, mix_w_in, sc_conv_w, ssd_conv_w,
           ssd_conv_b, ssd_dt_bias, ssd_a_log, ssd_d, ssd_norm, mix_w_out, ffn2_norm, ffn2_w_in,
           ffn2_w_out, ple_norm, ple_w_gate, ple_w_proj, final_norm):
    batch, seq, _ = x.shape
    depth = ffn1_norm.shape[0]
    t = batch * seq
    h = x.reshape(t, D_MODEL)
    e64 = _head_selector(SSD_HEAD_DIM, SCALE_SPLIT)
    row = lambda v: v.astype(F32)[None, :]

    for i in range(depth):
        h = _ffn(h, row(ffn1_norm[i]), ffn1_w_in[i].astype(BF16), ffn1_w_out[i].astype(BF16),
                 tm=1024, tf=512, name="ffn1")

        w_in_t = jnp.swapaxes(mix_w_in, 1, 2)[i]
        w_in = _transpose_cast(w_in_t, n_rows=MAIN_COLS, bn=512)
        w_dt = _transpose_cast(jnp.pad(w_in_t[MAIN_COLS:], ((0, LANES - SSD_HEADS), (0, 0))),
                               n_rows=LANES, bn=LANES)
        w_out = mix_w_out[i].astype(BF16)
        cw = ssd_conv_w[i].astype(F32)
        cb = ssd_conv_b[i].astype(F32)[None, :]
        h_ssd = _ssd_mix(h, row(mix_norm[i]), w_in, w_dt,
                         cw[:, :SSD_WIDTH], cw[:, SSD_WIDTH:], cb[:, :SSD_WIDTH], cb[:, SSD_WIDTH:],
                         _pad_lanes(ssd_dt_bias[i]), _pad_lanes(ssd_a_log[i]),
                         jnp.repeat(ssd_d[i].astype(F32), SSD_HEAD_DIM)[None, :],
                         row(ssd_norm[i]), e64, w_out, seq=seq)
        h = _conv_mix(h, h_ssd, row(mix_norm[i]), w_in, sc_conv_w[i].astype(F32), w_out, seq=seq)

        h = _ffn(h, row(ffn2_norm[i]), ffn2_w_in[i].astype(BF16), ffn2_w_out[i].astype(BF16),
                 tm=1024, tf=512, name="ffn2")

        h = _ple_final(h, p[i].reshape(t, -1), row(ple_norm[i]), ple_w_gate[i].astype(BF16),
                       ple_w_proj[i].astype(BF16), row(final_norm), tm=512,
                       last_layer=(i + 1 == depth))
    return h.reshape(batch, seq, D_MODEL)
```

```python
import functools

import jax
import jax.numpy as jnp
from jax import lax
from jax.experimental import pallas as pl
from jax.experimental.pallas import tpu as pltpu

F32 = jnp.float32
BF16 = jnp.bfloat16

D_MODEL = 2048
D_FF = 5632
CONV_WIDTH = 2048
SC_KERNEL = 3
SSD_HEADS = 32
SSD_HEAD_DIM = 64
SSD_WIDTH = SSD_HEADS * SSD_HEAD_DIM
SSD_STATE = 128
SSD_GROUPS = 4
SSD_CONV = 4
CHUNK = 128
SSD_BC = SSD_GROUPS * SSD_STATE
SSD_XBC = SSD_WIDTH + 2 * SSD_BC
EPS = 1e-6

LANES = 128
SUBLANES = 8
VMEM_LIMIT_BYTES = 60000 * 1024
GROUP_W = SSD_WIDTH // SSD_GROUPS
HEADS_PER_GROUP = SSD_HEADS // SSD_GROUPS
QUAD = 4
QUAD_W = QUAD * SSD_HEAD_DIM
LOG_SPLIT = 3
SCALE_SPLIT = 2

CONV_IN_COLS = 3 * CONV_WIDTH
MAIN_COLS = CONV_IN_COLS + SSD_WIDTH + SSD_XBC
SSD_IN_COLS = SSD_WIDTH + SSD_XBC + LANES
Z_OFF, XS_OFF = 0, SSD_WIDTH
BC_OFF = 2 * SSD_WIDTH
DT_OFF = 2 * SSD_WIDTH + 2 * SSD_BC
STEP_ROWS = 2 * CHUNK


def _params(n_axes):
    return pltpu.CompilerParams(
        dimension_semantics=("arbitrary",) * n_axes,
        vmem_limit_bytes=VMEM_LIMIT_BYTES)


def _rmsnorm(x, w):
    return x * lax.rsqrt(jnp.mean(x * x, axis=-1, keepdims=True) + EPS) * w


def _silu(x):
    return x * jax.nn.sigmoid(x)


def _resident(shape, block_index=None):
    index = (0,) * len(shape) if block_index is None else block_index
    return pl.BlockSpec(shape, lambda *_: index, pipeline_mode=pl.Buffered(1))


def _ffn_body(x_ref, nw_ref, wg_ref, wu_ref, wo_ref, o_ref, xn_ref):
    j = pl.program_id(1)

    @pl.when(j == 0)
    def _():
        xn_ref[...] = _rmsnorm(x_ref[...], nw_ref[...]).astype(BF16)
        o_ref[...] = jnp.zeros_like(o_ref)

    xn = xn_ref[...]
    g = jnp.dot(xn, wg_ref[...].astype(BF16), preferred_element_type=F32)
    u = jnp.dot(xn, wu_ref[...].astype(BF16), preferred_element_type=F32)
    a = (_silu(g) * u).astype(BF16)
    o_ref[...] += jnp.dot(a, wo_ref[...].astype(BF16), preferred_element_type=F32)

    @pl.when(j == pl.num_programs(1) - 1)
    def _():
        o_ref[...] = x_ref[...] + 0.5 * o_ref[...]


def _ffn(h, norm_w, w_in, w_out, *, tm, tf, name):
    t = h.shape[0]
    nj = D_FF // tf
    return pl.pallas_call(
        _ffn_body,
        grid=(t // tm, nj),
        in_specs=[
            pl.BlockSpec((tm, D_MODEL), lambda i, j: (i, 0), pipeline_mode=pl.Buffered(1)),
            pl.BlockSpec((1, D_MODEL), lambda i, j: (0, 0)),
            pl.BlockSpec((D_MODEL, tf), lambda i, j: (0, j)),
            pl.BlockSpec((D_MODEL, tf), lambda i, j: (0, j + nj)),
            pl.BlockSpec((tf, D_MODEL), lambda i, j: (j, 0)),
        ],
        out_specs=pl.BlockSpec((tm, D_MODEL), lambda i, j: (i, 0)),
        out_shape=jax.ShapeDtypeStruct((t, D_MODEL), F32),
        scratch_shapes=[pltpu.VMEM((tm, D_MODEL), BF16)],
        compiler_params=_params(2),
        name=name,
    )(h, norm_w, w_in, w_in, w_out)


def _causal_conv(buf_ref, w_ref, n_taps):
    ext = buf_ref[...]
    acc = None
    for k in range(n_taps):
        delay = n_taps - 1 - k
        shifted = ext if delay == 0 else pltpu.roll(ext, delay, 0)
        tap = shifted[SUBLANES:, :] * w_ref[k:k + 1, :]
        acc = tap if acc is None else acc + tap
    return acc


_BODY = pl.ds(SUBLANES, CHUNK)
_TAIL = pl.ds(CHUNK, SUBLANES)
_HEAD = pl.ds(0, SUBLANES)


def _in_proj(x, nw_ref, win_refs, z_ref):
    xn = _rmsnorm(x, nw_ref[...]).astype(BF16)
    col = 0
    for w_ref in win_refs:
        width = w_ref.shape[1]
        z_ref[:, col:col + width] = jnp.dot(xn, w_ref[...], preferred_element_type=F32)
        col += width


def _chunk_pipeline(x_ref, xnext_ref, nw_ref, win_refs, z_a, z_b, mix_chunk):
    @pl.when(pl.program_id(0) == 0)
    def _():
        _in_proj(x_ref[0:CHUNK, :], nw_ref, win_refs, z_a)

    _in_proj(x_ref[CHUNK:STEP_ROWS, :], nw_ref, win_refs, z_b)
    mix_chunk(z_a, pl.ds(0, CHUNK))
    _in_proj(xnext_ref[...], nw_ref, win_refs, z_a)
    mix_chunk(z_b, pl.ds(CHUNK, CHUNK))


def _chunk_specs(t):
    last_chunk = t // CHUNK - 1
    cur = pl.BlockSpec((STEP_ROWS, D_MODEL), lambda s: (s, 0))
    nxt = pl.BlockSpec((CHUNK, D_MODEL), lambda s: (jnp.minimum(2 * s + 2, last_chunk), 0))
    return cur, nxt


def _conv_mix_body(x_ref, xnext_ref, res_ref, nw_ref, win_ref, scw_ref, wout_ref, o_ref,
                   z_a, z_b, pbuf, *, steps_per_seq):
    @pl.when(pl.program_id(0) % steps_per_seq == 0)
    def _():
        pbuf[_HEAD, :] = jnp.zeros((SUBLANES, CONV_WIDTH), F32)

    def mix_chunk(z_ref, rows):
        pbuf[_BODY, :] = z_ref[:, CONV_WIDTH:2 * CONV_WIDTH] * z_ref[:, 2 * CONV_WIDTH:]
        y_a = z_ref[:, 0:CONV_WIDTH] * _causal_conv(pbuf, scw_ref, SC_KERNEL)
        pbuf[_HEAD, :] = pbuf[_TAIL, :]
        o_ref[rows, :] = res_ref[rows, :] + jnp.dot(
            y_a.astype(BF16), wout_ref[...], preferred_element_type=F32)

    _chunk_pipeline(x_ref, xnext_ref, nw_ref, [win_ref], z_a, z_b, mix_chunk)


def _conv_mix(h, h_res, norm_w, w_in, sc_w, w_out, *, seq):
    t = h.shape[0]
    cur, nxt = _chunk_specs(t)
    return pl.pallas_call(
        functools.partial(_conv_mix_body, steps_per_seq=seq // STEP_ROWS),
        grid=(t // STEP_ROWS,),
        in_specs=[
            cur, nxt, cur,
            _resident((1, D_MODEL)),
            _resident((D_MODEL, CONV_IN_COLS), (0, 0)),
            _resident((SC_KERNEL, CONV_WIDTH)),
            _resident((CONV_WIDTH, D_MODEL), (0, 0)),
        ],
        out_specs=pl.BlockSpec((STEP_ROWS, D_MODEL), lambda s: (s, 0)),
        out_shape=jax.ShapeDtypeStruct((t, D_MODEL), F32),
        scratch_shapes=[
            pltpu.VMEM((CHUNK, CONV_IN_COLS), F32),
            pltpu.VMEM((CHUNK, CONV_IN_COLS), F32),
            pltpu.VMEM((CHUNK + SUBLANES, CONV_WIDTH), F32),
        ],
        compiler_params=_params(1),
        name="conv_mix",
    )(h, h, h_res, norm_w, w_in, sc_w, w_out)


def _split_cat(a, axis, n_terms):
    terms = []
    rest = a
    for _ in range(n_terms):
        term = rest.astype(BF16)
        terms.append(term)
        rest = rest - term.astype(F32)
    return jnp.concatenate(terms, axis=axis)


def _ssd_chunk(z_ref, cwx_ref, cwbc_ref, cbx_ref, cbbc_ref, dtb_ref, alog_ref, dsk_ref,
               gnw_ref, e64_ref, xbuf, bcbuf, st_ref):
    xbuf[_BODY, :] = z_ref[:, XS_OFF:XS_OFF + SSD_WIDTH]
    bcbuf[_BODY, :] = z_ref[:, BC_OFF:BC_OFF + 2 * SSD_BC]
    xc = _silu(_causal_conv(xbuf, cwx_ref, SSD_CONV) + cbx_ref[...])
    bcc = _silu(_causal_conv(bcbuf, cwbc_ref, SSD_CONV) + cbbc_ref[...])
    xbuf[_HEAD, :] = xbuf[_TAIL, :]
    bcbuf[_HEAD, :] = bcbuf[_TAIL, :]

    dt_in = z_ref[:, DT_OFF:DT_OFF + LANES] + dtb_ref[...]
    dtv = jnp.maximum(dt_in, 0.0) + jnp.log1p(jnp.exp(-jnp.abs(dt_in)))
    a_neg = -jnp.exp(alog_ref[...])
    row = lax.broadcasted_iota(jnp.int32, (CHUNK, CHUNK), 0)
    col = lax.broadcasted_iota(jnp.int32, (CHUNK, CHUNK), 1)
    causal = col <= row
    tril = causal.astype(BF16)
    acs = jnp.dot(jnp.concatenate([tril] * LOG_SPLIT, axis=1),
                  _split_cat(dtv * a_neg, 0, LOG_SPLIT),
                  preferred_element_type=F32)
    acs_t = acs.T
    decay = jnp.exp(acs[CHUNK - 1:CHUNK, :] - acs)
    exp_acs = jnp.exp(acs)

    per_head = jnp.concatenate([dtv, dtv * decay, exp_acs], axis=0)
    per_chan = jnp.dot(_split_cat(per_head, 1, SCALE_SPLIT), e64_ref[...],
                       preferred_element_type=F32)
    dt_e = per_chan[0:CHUNK]
    dt_dec_e = per_chan[CHUNK:2 * CHUNK]
    exp_acs_e = per_chan[2 * CHUNK:3 * CHUNK]
    chunk_decay = exp_acs_e[CHUNK - 1:CHUNK, :]

    x_dt_b = (xc * dt_e).astype(BF16)
    x_dec_b = (xc * dt_dec_e).astype(BF16)

    lane_head = lax.broadcasted_iota(jnp.int32, (CHUNK, QUAD_W), 1) // SSD_HEAD_DIM
    y_groups = []
    for g in range(SSD_GROUPS):
        gsl = slice(g * GROUP_W, (g + 1) * GROUP_W)
        b_g = bcc[:, g * SSD_STATE:(g + 1) * SSD_STATE]
        c_g = bcc[:, SSD_BC + g * SSD_STATE:SSD_BC + (g + 1) * SSD_STATE]
        b_gb = b_g.astype(BF16)
        c_gb = c_g.astype(BF16)
        cb = lax.dot_general(c_gb, b_gb, (((1,), (1,)), ((), ())),
                             preferred_element_type=F32)
        cb = jnp.where(causal, cb, 0.0)
        states = jnp.dot(b_g.T.astype(BF16), x_dec_b[:, gsl], preferred_element_type=F32)
        prev = st_ref[:, gsl]
        y_off = jnp.dot(c_gb, prev.astype(BF16), preferred_element_type=F32) * exp_acs_e[:, gsl]
        st_ref[:, gsl] = prev * chunk_decay[:, gsl] + states
        y_diag = []
        for q in range(HEADS_PER_GROUP // QUAD):
            h0 = g * HEADS_PER_GROUP + q * QUAD
            m_parts = []
            x_parts = []
            x_q = x_dt_b[:, h0 * SSD_HEAD_DIM:h0 * SSD_HEAD_DIM + QUAD_W]
            for hh in range(QUAD):
                h = h0 + hh
                diff = jnp.broadcast_to(acs[:, h:h + 1], (CHUNK, CHUNK)) - acs_t[h:h + 1, :]
                m_parts.append((cb * jnp.exp(jnp.where(causal, diff, 0.0))).astype(BF16))
                x_parts.append(jnp.where(lane_head == hh, x_q, jnp.zeros_like(x_q)))
            m_q = jnp.concatenate(m_parts, axis=1)
            x_bd = jnp.concatenate(x_parts, axis=0)
            y_diag.append(jnp.dot(m_q, x_bd, preferred_element_type=F32))
        y_groups.append(jnp.concatenate(y_diag, axis=1) + y_off)
    y = jnp.concatenate(y_groups, axis=1) + dsk_ref[...] * xc

    gz = y * _silu(z_ref[:, Z_OFF:Z_OFF + SSD_WIDTH])
    gnw = gnw_ref[...]
    out = []
    for g in range(SSD_GROUPS):
        gsl = slice(g * GROUP_W, (g + 1) * GROUP_W)
        gg = gz[:, gsl]
        out.append(gg * lax.rsqrt(jnp.mean(gg * gg, axis=-1, keepdims=True) + EPS) * gnw[:, gsl])
    return jnp.concatenate(out, axis=1)


def _ssd_mix_body(x_ref, xnext_ref, nw_ref, wz_ref, wx0_ref, wx1_ref, wbc_ref, wdt_ref,
                  cwx_ref, cwbc_ref, cbx_ref, cbbc_ref,
                  dtb_ref, alog_ref, dsk_ref, gnw_ref, e64_ref, wout_ref, o_ref,
                  z_a, z_b, xbuf, bcbuf, st_ref, *, steps_per_seq):
    @pl.when(pl.program_id(0) % steps_per_seq == 0)
    def _():
        xbuf[_HEAD, :] = jnp.zeros((SUBLANES, SSD_WIDTH), F32)
        bcbuf[_HEAD, :] = jnp.zeros((SUBLANES, 2 * SSD_BC), F32)
        st_ref[...] = jnp.zeros_like(st_ref)

    def mix_chunk(z_ref, rows):
        y_b = _ssd_chunk(z_ref, cwx_ref, cwbc_ref, cbx_ref, cbbc_ref, dtb_ref, alog_ref,
                         dsk_ref, gnw_ref, e64_ref, xbuf, bcbuf, st_ref)
        o_ref[rows, :] = x_ref[rows, :] + jnp.dot(
            y_b.astype(BF16), wout_ref[...], preferred_element_type=F32)

    win_refs = [wz_ref, wx0_ref, wx1_ref, wbc_ref, wdt_ref]
    _chunk_pipeline(x_ref, xnext_ref, nw_ref, win_refs, z_a, z_b, mix_chunk)


def _ssd_mix(h, norm_w, w_in, w_dt, cw_x, cw_bc, cb_x, cb_bc, dt_bias, a_log, d_skip,
             gnorm_w, e64, w_out, *, seq):
    t = h.shape[0]
    half = SSD_WIDTH // 2
    cur, nxt = _chunk_specs(t)
    return pl.pallas_call(
        functools.partial(_ssd_mix_body, steps_per_seq=seq // STEP_ROWS),
        grid=(t // STEP_ROWS,),
        in_specs=[
            cur, nxt,
            _resident((1, D_MODEL)),
            _resident((D_MODEL, SSD_WIDTH), (0, CONV_IN_COLS // SSD_WIDTH)),
            _resident((D_MODEL, half), (0, (CONV_IN_COLS + SSD_WIDTH) // half)),
            _resident((D_MODEL, half), (0, (CONV_IN_COLS + SSD_WIDTH) // half + 1)),
            _resident((D_MODEL, 2 * SSD_BC), (0, (CONV_IN_COLS + 2 * SSD_WIDTH) // (2 * SSD_BC))),
            _resident((D_MODEL, LANES)),
            _resident((SSD_CONV, SSD_WIDTH)),
            _resident((SSD_CONV, 2 * SSD_BC)),
            _resident((1, SSD_WIDTH)),
            _resident((1, 2 * SSD_BC)),
            _resident((1, LANES)),
            _resident((1, LANES)),
            _resident((1, SSD_WIDTH)),
            _resident((1, SSD_WIDTH)),
            _resident((SCALE_SPLIT * LANES, SSD_WIDTH)),
            _resident((SSD_WIDTH, D_MODEL), (1, 0)),
        ],
        out_specs=pl.BlockSpec((STEP_ROWS, D_MODEL), lambda s: (s, 0)),
        out_shape=jax.ShapeDtypeStruct((t, D_MODEL), F32),
        scratch_shapes=[
            pltpu.VMEM((CHUNK, SSD_IN_COLS), F32),
            pltpu.VMEM((CHUNK, SSD_IN_COLS), F32),
            pltpu.VMEM((CHUNK + SUBLANES, SSD_WIDTH), F32),
            pltpu.VMEM((CHUNK + SUBLANES, 2 * SSD_BC), F32),
            pltpu.VMEM((SSD_STATE, SSD_WIDTH), F32),
        ],
        compiler_params=_params(1),
        name="ssd_mix",
    )(h, h, norm_w, w_in, w_in, w_in, w_in, w_dt, cw_x, cw_bc, cb_x, cb_bc, dt_bias, a_log,
      d_skip, gnorm_w, e64, w_out)


def _transpose_cast_body(w_ref, o_ref):
    o_ref[...] = w_ref[...].T.astype(BF16)


def _transpose_cast(w_t, *, n_rows, bn):
    d = w_t.shape[1]
    return pl.pallas_call(
        _transpose_cast_body,
        grid=(n_rows // bn,),
        in_specs=[pl.BlockSpec((bn, d), lambda j: (j, 0))],
        out_specs=pl.BlockSpec((d, bn), lambda j: (0, j)),
        out_shape=jax.ShapeDtypeStruct((d, n_rows), BF16),
        compiler_params=_params(1),
        name="transpose_cast",
    )(w_t)


def _ple_final_body(h_ref, p_ref, pnw_ref, wg_ref, wp_ref, fnw_ref, o_ref, *, last_layer):
    h = h_ref[...]
    xn = _rmsnorm(h, pnw_ref[...]).astype(BF16)
    gate = jax.nn.sigmoid(jnp.dot(xn, wg_ref[...], preferred_element_type=F32))
    proj = jnp.dot(p_ref[...].astype(BF16), wp_ref[...], preferred_element_type=F32)
    h = h + gate * proj
    o_ref[...] = _rmsnorm(h, fnw_ref[...]) if last_layer else h


def _ple_final(h, p, ple_norm, w_gate, w_proj, final_norm, *, tm, last_layer):
    t = h.shape[0]
    ple_dim = p.shape[1]
    return pl.pallas_call(
        functools.partial(_ple_final_body, last_layer=last_layer),
        grid=(t // tm,),
        in_specs=[
            pl.BlockSpec((tm, D_MODEL), lambda i: (i, 0)),
            pl.BlockSpec((tm, ple_dim), lambda i: (i, 0)),
            pl.BlockSpec((1, D_MODEL), lambda i: (0, 0)),
            pl.BlockSpec((D_MODEL, D_MODEL), lambda i: (0, 0)),
            pl.BlockSpec((ple_dim, D_MODEL), lambda i: (0, 0)),
            pl.BlockSpec((1, D_MODEL), lambda i: (0, 0)),
        ],
        out_specs=pl.BlockSpec((tm, D_MODEL), lambda i: (i, 0)),
        out_shape=jax.ShapeDtypeStruct((t, D_MODEL), F32),
        compiler_params=_params(1),
        name="ple_final",
    )(h, p, ple_norm, w_gate, w_proj, final_norm)


def _head_selector(width, n_terms):
    src = jnp.arange(LANES, dtype=jnp.int32)[:, None]
    dst = jnp.arange(SSD_HEADS * width, dtype=jnp.int32)[None, :] // width
    return jnp.tile((src == dst).astype(BF16), (n_terms, 1))


def _pad_lanes(v):
    return jnp.pad(v.astype(F32), (0, LANES - v.shape[0]))[None, :]


def kernel(x, p, ffn1_norm, ffn1_w_in, ffn1_w_out, mix_norm, mix_w_in, sc_conv_w, ssd_conv_w,
           ssd_conv_b, ssd_dt_bias, ssd_a_log, ssd_d, ssd_norm, mix_w_out, ffn2_norm, ffn2_w_in,
           ffn2_w_out, ple_norm, ple_w_gate, ple_w_proj, final_norm):
    batch, seq, _ = x.shape
    depth = ffn1_norm.shape[0]
    t = batch * seq
    h = x.reshape(t, D_MODEL)
    e64 = _head_selector(SSD_HEAD_DIM, SCALE_SPLIT)
    row = lambda v: v.astype(F32)[None, :]

    for i in range(depth):
        h = _ffn(h, row(ffn1_norm[i]), ffn1_w_in[i], ffn1_w_out[i], tm=1024, tf=512, name="ffn1")

        w_in_t = jnp.swapaxes(mix_w_in, 1, 2)[i]
        w_in = _transpose_cast(w_in_t, n_rows=MAIN_COLS, bn=512)
        w_dt = _transpose_cast(jnp.pad(w_in_t[MAIN_COLS:], ((0, LANES - SSD_HEADS), (0, 0))),
                               n_rows=LANES, bn=LANES)
        w_out = mix_w_out[i].astype(BF16)
        cw = ssd_conv_w[i].astype(F32)
        cb = ssd_conv_b[i].astype(F32)[None, :]
        h_ssd = _ssd_mix(h, row(mix_norm[i]), w_in, w_dt,
                         cw[:, :SSD_WIDTH], cw[:, SSD_WIDTH:], cb[:, :SSD_WIDTH], cb[:, SSD_WIDTH:],
                         _pad_lanes(ssd_dt_bias[i]), _pad_lanes(ssd_a_log[i]),
                         jnp.repeat(ssd_d[i].astype(F32), SSD_HEAD_DIM)[None, :],
                         row(ssd_norm[i]), e64, w_out, seq=seq)
        h = _conv_mix(h, h_ssd, row(mix_norm[i]), w_in, sc_conv_w[i].astype(F32), w_out, seq=seq)

        h = _ffn(h, row(ffn2_norm[i]), ffn2_w_in[i], ffn2_w_out[i], tm=1024, tf=512, name="ffn2")

        h = _ple_final(h, p[i].reshape(t, -1), row(ple_norm[i]), ple_w_gate[i].astype(BF16),
                       ple_w_proj[i].astype(BF16), row(final_norm), tm=512,
                       last_layer=(i + 1 == depth))
    return h.reshape(batch, seq, D_MODEL)
```

```python
import functools

import jax
import jax.numpy as jnp
from jax import lax
from jax.experimental import pallas as pl
from jax.experimental.pallas import tpu as pltpu

F32 = jnp.float32
BF16 = jnp.bfloat16

D_MODEL = 2048
D_FF = 5632
CONV_WIDTH = 2048
SC_KERNEL = 3
SSD_HEADS = 32
SSD_HEAD_DIM = 64
SSD_WIDTH = SSD_HEADS * SSD_HEAD_DIM
SSD_STATE = 128
SSD_GROUPS = 4
SSD_CONV = 4
CHUNK = 128
SSD_BC = SSD_GROUPS * SSD_STATE
SSD_XBC = SSD_WIDTH + 2 * SSD_BC
EPS = 1e-6

LANES = 128
SUBLANES = 8
VMEM_LIMIT_BYTES = 60000 * 1024
FFN_VMEM_LIMIT_BYTES = 62 * 1024 * 1024
GROUP_W = SSD_WIDTH // SSD_GROUPS
HEADS_PER_GROUP = SSD_HEADS // SSD_GROUPS
QUAD = 4
QUAD_W = QUAD * SSD_HEAD_DIM
LOG_SPLIT = 3
SCALE_SPLIT = 2

CONV_IN_COLS = 3 * CONV_WIDTH
MAIN_COLS = CONV_IN_COLS + SSD_WIDTH + SSD_XBC
SSD_IN_COLS = SSD_WIDTH + SSD_XBC + LANES
Z_OFF, XS_OFF = 0, SSD_WIDTH
BC_OFF = 2 * SSD_WIDTH
DT_OFF = 2 * SSD_WIDTH + 2 * SSD_BC
STEP_ROWS = 2 * CHUNK


def _params(n_axes, vmem_limit_bytes=VMEM_LIMIT_BYTES):
    return pltpu.CompilerParams(
        dimension_semantics=("arbitrary",) * n_axes,
        vmem_limit_bytes=vmem_limit_bytes)


def _rmsnorm(x, w):
    return x * lax.rsqrt(jnp.mean(x * x, axis=-1, keepdims=True) + EPS) * w


def _silu(x):
    return x * jax.nn.sigmoid(x)


def _resident(shape, block_index=None):
    index = (0,) * len(shape) if block_index is None else block_index
    return pl.BlockSpec(shape, lambda *_: index, pipeline_mode=pl.Buffered(1))


def _ffn_body(x_hbm, nw_ref, wg_ref, wu_ref, wo_ref, o_ref, xn_ref, xbuf, sem, *, tm):
    i = pl.program_id(0)
    j = pl.program_id(1)

    def x_copy(tile):
        return pltpu.make_async_copy(x_hbm.at[pl.ds(tile * tm, tm), :], xbuf, sem)

    @pl.when((i == 0) & (j == 0))
    def _():
        x_copy(0).start()

    @pl.when(j == 0)
    def _():
        x_copy(i).wait()
        nw = nw_ref[...]

        def slab(r, carry):
            rows = pl.ds(pl.multiple_of(r * CHUNK, CHUNK), CHUNK)
            x = xbuf[rows, :]
            xn_ref[rows, :] = _rmsnorm(x, nw).astype(BF16)
            o_ref[rows, :] = 2.0 * x
            return carry

        lax.fori_loop(0, tm // CHUNK, slab, None)

    @pl.when((j == 1) & (i + 1 < pl.num_programs(0)))
    def _():
        x_copy(i + 1).start()

    xn = xn_ref[...]
    g = jnp.dot(xn, wg_ref[...].astype(BF16), preferred_element_type=F32)
    u = jnp.dot(xn, wu_ref[...].astype(BF16), preferred_element_type=F32)
    a = (_silu(g) * u).astype(BF16)
    o_ref[...] += jnp.dot(a, wo_ref[...].astype(BF16), preferred_element_type=F32)

    @pl.when(j == pl.num_programs(1) - 1)
    def _():
        o_ref[...] = 0.5 * o_ref[...]


def _ffn(h, norm_w, w_in, w_out, *, tm, tf, name):
    t = h.shape[0]
    nj = D_FF // tf
    assert nj >= 2, "the x prefetch is issued in the second d_ff step"
    return pl.pallas_call(
        functools.partial(_ffn_body, tm=tm),
        grid=(t // tm, nj),
        in_specs=[
            pl.BlockSpec(memory_space=pl.ANY),
            pl.BlockSpec((1, D_MODEL), lambda i, j: (0, 0)),
            pl.BlockSpec((D_MODEL, tf), lambda i, j: (0, j)),
            pl.BlockSpec((D_MODEL, tf), lambda i, j: (0, j + nj)),
            pl.BlockSpec((tf, D_MODEL), lambda i, j: (j, 0)),
        ],
        out_specs=pl.BlockSpec((tm, D_MODEL), lambda i, j: (i, 0)),
        out_shape=jax.ShapeDtypeStruct((t, D_MODEL), F32),
        scratch_shapes=[
            pltpu.VMEM((tm, D_MODEL), BF16),
            pltpu.VMEM((tm, D_MODEL), F32),
            pltpu.SemaphoreType.DMA(()),
        ],
        compiler_params=_params(2, FFN_VMEM_LIMIT_BYTES),
        name=name,
    )(h, norm_w, w_in, w_in, w_out)


def _causal_conv(buf_ref, w_ref, n_taps):
    ext = buf_ref[...]
    acc = None
    for k in range(n_taps):
        delay = n_taps - 1 - k
        shifted = ext if delay == 0 else pltpu.roll(ext, delay, 0)
        tap = shifted[SUBLANES:, :] * w_ref[k:k + 1, :]
        acc = tap if acc is None else acc + tap
    return acc


_BODY = pl.ds(SUBLANES, CHUNK)
_TAIL = pl.ds(CHUNK, SUBLANES)
_HEAD = pl.ds(0, SUBLANES)


def _in_proj(x, nw_ref, win_refs, z_ref):
    xn = _rmsnorm(x, nw_ref[...]).astype(BF16)
    col = 0
    for w_ref in win_refs:
        width = w_ref.shape[1]
        z_ref[:, col:col + width] = jnp.dot(xn, w_ref[...], preferred_element_type=F32)
        col += width


def _chunk_pipeline(x_ref, xnext_ref, nw_ref, win_refs, z_a, z_b, mix_chunk):
    @pl.when(pl.program_id(0) == 0)
    def _():
        _in_proj(x_ref[0:CHUNK, :], nw_ref, win_refs, z_a)

    _in_proj(x_ref[CHUNK:STEP_ROWS, :], nw_ref, win_refs, z_b)
    mix_chunk(z_a, pl.ds(0, CHUNK))
    _in_proj(xnext_ref[...], nw_ref, win_refs, z_a)
    mix_chunk(z_b, pl.ds(CHUNK, CHUNK))


def _chunk_specs(t):
    last_chunk = t // CHUNK - 1
    cur = pl.BlockSpec((STEP_ROWS, D_MODEL), lambda s: (s, 0))
    nxt = pl.BlockSpec((CHUNK, D_MODEL), lambda s: (jnp.minimum(2 * s + 2, last_chunk), 0))
    return cur, nxt


def _conv_mix_body(x_ref, xnext_ref, res_ref, nw_ref, win_ref, scw_ref, wout_ref, o_ref,
                   z_a, z_b, pbuf, *, steps_per_seq):
    @pl.when(pl.program_id(0) % steps_per_seq == 0)
    def _():
        pbuf[_HEAD, :] = jnp.zeros((SUBLANES, CONV_WIDTH), F32)

    def mix_chunk(z_ref, rows):
        pbuf[_BODY, :] = z_ref[:, CONV_WIDTH:2 * CONV_WIDTH] * z_ref[:, 2 * CONV_WIDTH:]
        y_a = z_ref[:, 0:CONV_WIDTH] * _causal_conv(pbuf, scw_ref, SC_KERNEL)
        pbuf[_HEAD, :] = pbuf[_TAIL, :]
        o_ref[rows, :] = res_ref[rows, :] + jnp.dot(
            y_a.astype(BF16), wout_ref[...], preferred_element_type=F32)

    _chunk_pipeline(x_ref, xnext_ref, nw_ref, [win_ref], z_a, z_b, mix_chunk)


def _conv_mix(h, h_res, norm_w, w_in, sc_w, w_out, *, seq):
    t = h.shape[0]
    cur, nxt = _chunk_specs(t)
    return pl.pallas_call(
        functools.partial(_conv_mix_body, steps_per_seq=seq // STEP_ROWS),
        grid=(t // STEP_ROWS,),
        in_specs=[
            cur, nxt, cur,
            _resident((1, D_MODEL)),
            _resident((D_MODEL, CONV_IN_COLS), (0, 0)),
            _resident((SC_KERNEL, CONV_WIDTH)),
            _resident((CONV_WIDTH, D_MODEL), (0, 0)),
        ],
        out_specs=pl.BlockSpec((STEP_ROWS, D_MODEL), lambda s: (s, 0)),
        out_shape=jax.ShapeDtypeStruct((t, D_MODEL), F32),
        scratch_shapes=[
            pltpu.VMEM((CHUNK, CONV_IN_COLS), F32),
            pltpu.VMEM((CHUNK, CONV_IN_COLS), F32),
            pltpu.VMEM((CHUNK + SUBLANES, CONV_WIDTH), F32),
        ],
        compiler_params=_params(1),
        name="conv_mix",
    )(h, h, h_res, norm_w, w_in, sc_w, w_out)


def _split_cat(a, axis, n_terms):
    terms = []
    rest = a
    for _ in range(n_terms):
        term = rest.astype(BF16)
        terms.append(term)
        rest = rest - term.astype(F32)
    return jnp.concatenate(terms, axis=axis)


def _ssd_chunk(z_ref, cwx_ref, cwbc_ref, cbx_ref, cbbc_ref, dtb_ref, alog_ref, dsk_ref,
               gnw_ref, e64_ref, xbuf, bcbuf, st_ref):
    xbuf[_BODY, :] = z_ref[:, XS_OFF:XS_OFF + SSD_WIDTH]
    bcbuf[_BODY, :] = z_ref[:, BC_OFF:BC_OFF + 2 * SSD_BC]
    xc = _silu(_causal_conv(xbuf, cwx_ref, SSD_CONV) + cbx_ref[...])
    bcc = _silu(_causal_conv(bcbuf, cwbc_ref, SSD_CONV) + cbbc_ref[...])
    xbuf[_HEAD, :] = xbuf[_TAIL, :]
    bcbuf[_HEAD, :] = bcbuf[_TAIL, :]

    dt_in = z_ref[:, DT_OFF:DT_OFF + LANES] + dtb_ref[...]
    dtv = jnp.maximum(dt_in, 0.0) + jnp.log1p(jnp.exp(-jnp.abs(dt_in)))
    a_neg = -jnp.exp(alog_ref[...])
    row = lax.broadcasted_iota(jnp.int32, (CHUNK, CHUNK), 0)
    col = lax.broadcasted_iota(jnp.int32, (CHUNK, CHUNK), 1)
    causal = col <= row
    tril = causal.astype(BF16)
    acs = jnp.dot(jnp.concatenate([tril] * LOG_SPLIT, axis=1),
                  _split_cat(dtv * a_neg, 0, LOG_SPLIT),
                  preferred_element_type=F32)
    acs_t = acs.T
    decay = jnp.exp(acs[CHUNK - 1:CHUNK, :] - acs)
    exp_acs = jnp.exp(acs)

    per_head = jnp.concatenate([dtv, dtv * decay, exp_acs], axis=0)
    per_chan = jnp.dot(_split_cat(per_head, 1, SCALE_SPLIT), e64_ref[...],
                       preferred_element_type=F32)
    dt_e = per_chan[0:CHUNK]
    dt_dec_e = per_chan[CHUNK:2 * CHUNK]
    exp_acs_e = per_chan[2 * CHUNK:3 * CHUNK]
    chunk_decay = exp_acs_e[CHUNK - 1:CHUNK, :]

    x_dt_b = (xc * dt_e).astype(BF16)
    x_dec_b = (xc * dt_dec_e).astype(BF16)

    lane_head = lax.broadcasted_iota(jnp.int32, (CHUNK, QUAD_W), 1) // SSD_HEAD_DIM
    y_groups = []
    for g in range(SSD_GROUPS):
        gsl = slice(g * GROUP_W, (g + 1) * GROUP_W)
        b_g = bcc[:, g * SSD_STATE:(g + 1) * SSD_STATE]
        c_g = bcc[:, SSD_BC + g * SSD_STATE:SSD_BC + (g + 1) * SSD_STATE]
        b_gb = b_g.astype(BF16)
        c_gb = c_g.astype(BF16)
        cb = lax.dot_general(c_gb, b_gb, (((1,), (1,)), ((), ())),
                             preferred_element_type=F32)
        cb = jnp.where(causal, cb, 0.0)
        states = jnp.dot(b_g.T.astype(BF16), x_dec_b[:, gsl], preferred_element_type=F32)
        prev = st_ref[:, gsl]
        y_off = jnp.dot(c_gb, prev.astype(BF16), preferred_element_type=F32) * exp_acs_e[:, gsl]
        st_ref[:, gsl] = prev * chunk_decay[:, gsl] + states
        y_diag = []
        for q in range(HEADS_PER_GROUP // QUAD):
            h0 = g * HEADS_PER_GROUP + q * QUAD
            m_parts = []
            x_parts = []
            x_q = x_dt_b[:, h0 * SSD_HEAD_DIM:h0 * SSD_HEAD_DIM + QUAD_W]
            for hh in range(QUAD):
                h = h0 + hh
                diff = jnp.broadcast_to(acs[:, h:h + 1], (CHUNK, CHUNK)) - acs_t[h:h + 1, :]
                m_parts.append((cb * jnp.exp(jnp.where(causal, diff, 0.0))).astype(BF16))
                x_parts.append(jnp.where(lane_head == hh, x_q, jnp.zeros_like(x_q)))
            m_q = jnp.concatenate(m_parts, axis=1)
            x_bd = jnp.concatenate(x_parts, axis=0)
            y_diag.append(jnp.dot(m_q, x_bd, preferred_element_type=F32))
        y_groups.append(jnp.concatenate(y_diag, axis=1) + y_off)
    y = jnp.concatenate(y_groups, axis=1) + dsk_ref[...] * xc

    gz = y * _silu(z_ref[:, Z_OFF:Z_OFF + SSD_WIDTH])
    gnw = gnw_ref[...]
    out = []
    for g in range(SSD_GROUPS):
        gsl = slice(g * GROUP_W, (g + 1) * GROUP_W)
        gg = gz[:, gsl]
        out.append(gg * lax.rsqrt(jnp.mean(gg * gg, axis=-1, keepdims=True) + EPS) * gnw[:, gsl])
    return jnp.concatenate(out, axis=1)


def _ssd_mix_body(x_ref, xnext_ref, nw_ref, wz_ref, wx0_ref, wx1_ref, wbc_ref, wdt_ref,
                  cwx_ref, cwbc_ref, cbx_ref, cbbc_ref,
                  dtb_ref, alog_ref, dsk_ref, gnw_ref, e64_ref, wout_ref, o_ref,
                  z_a, z_b, xbuf, bcbuf, st_ref, *, steps_per_seq):
    @pl.when(pl.program_id(0) % steps_per_seq == 0)
    def _():
        xbuf[_HEAD, :] = jnp.zeros((SUBLANES, SSD_WIDTH), F32)
        bcbuf[_HEAD, :] = jnp.zeros((SUBLANES, 2 * SSD_BC), F32)
        st_ref[...] = jnp.zeros_like(st_ref)

    def mix_chunk(z_ref, rows):
        y_b = _ssd_chunk(z_ref, cwx_ref, cwbc_ref, cbx_ref, cbbc_ref, dtb_ref, alog_ref,
                         dsk_ref, gnw_ref, e64_ref, xbuf, bcbuf, st_ref)
        o_ref[rows, :] = x_ref[rows, :] + jnp.dot(
            y_b.astype(BF16), wout_ref[...], preferred_element_type=F32)

    win_refs = [wz_ref, wx0_ref, wx1_ref, wbc_ref, wdt_ref]
    _chunk_pipeline(x_ref, xnext_ref, nw_ref, win_refs, z_a, z_b, mix_chunk)


def _ssd_mix(h, norm_w, w_in, w_dt, cw_x, cw_bc, cb_x, cb_bc, dt_bias, a_log, d_skip,
             gnorm_w, e64, w_out, *, seq):
    t = h.shape[0]
    half = SSD_WIDTH // 2
    cur, nxt = _chunk_specs(t)
    return pl.pallas_call(
        functools.partial(_ssd_mix_body, steps_per_seq=seq // STEP_ROWS),
        grid=(t // STEP_ROWS,),
        in_specs=[
            cur, nxt,
            _resident((1, D_MODEL)),
            _resident((D_MODEL, SSD_WIDTH), (0, CONV_IN_COLS // SSD_WIDTH)),
            _resident((D_MODEL, half), (0, (CONV_IN_COLS + SSD_WIDTH) // half)),
            _resident((D_MODEL, half), (0, (CONV_IN_COLS + SSD_WIDTH) // half + 1)),
            _resident((D_MODEL, 2 * SSD_BC), (0, (CONV_IN_COLS + 2 * SSD_WIDTH) // (2 * SSD_BC))),
            _resident((D_MODEL, LANES)),
            _resident((SSD_CONV, SSD_WIDTH)),
            _resident((SSD_CONV, 2 * SSD_BC)),
            _resident((1, SSD_WIDTH)),
            _resident((1, 2 * SSD_BC)),
            _resident((1, LANES)),
            _resident((1, LANES)),
            _resident((1, SSD_WIDTH)),
            _resident((1, SSD_WIDTH)),
            _resident((SCALE_SPLIT * LANES, SSD_WIDTH)),
            _resident((SSD_WIDTH, D_MODEL), (1, 0)),
        ],
        out_specs=pl.BlockSpec((STEP_ROWS, D_MODEL), lambda s: (s, 0)),
        out_shape=jax.ShapeDtypeStruct((t, D_MODEL), F32),
        scratch_shapes=[
            pltpu.VMEM((CHUNK, SSD_IN_COLS), F32),
            pltpu.VMEM((CHUNK, SSD_IN_COLS), F32),
            pltpu.VMEM((CHUNK + SUBLANES, SSD_WIDTH), F32),
            pltpu.VMEM((CHUNK + SUBLANES, 2 * SSD_BC), F32),
            pltpu.VMEM((SSD_STATE, SSD_WIDTH), F32),
        ],
        compiler_params=_params(1),
        name="ssd_mix",
    )(h, h, norm_w, w_in, w_in, w_in, w_in, w_dt, cw_x, cw_bc, cb_x, cb_bc, dt_bias, a_log,
      d_skip, gnorm_w, e64, w_out)


def _transpose_cast_body(w_ref, o_ref):
    o_ref[...] = w_ref[...].T.astype(BF16)


def _transpose_cast(w_t, *, n_rows, bn):
    d = w_t.shape[1]
    return pl.pallas_call(
        _transpose_cast_body,
        grid=(n_rows // bn,),
        in_specs=[pl.BlockSpec((bn, d), lambda j: (j, 0))],
        out_specs=pl.BlockSpec((d, bn), lambda j: (0, j)),
        out_shape=jax.ShapeDtypeStruct((d, n_rows), BF16),
        compiler_params=_params(1),
        name="transpose_cast",
    )(w_t)


def _ple_final_body(h_ref, p_ref, pnw_ref, wg_ref, wp_ref, fnw_ref, o_ref, *, last_layer):
    h = h_ref[...]
    xn = _rmsnorm(h, pnw_ref[...]).astype(BF16)
    gate = jax.nn.sigmoid(jnp.dot(xn, wg_ref[...], preferred_element_type=F32))
    proj = jnp.dot(p_ref[...].astype(BF16), wp_ref[...], preferred_element_type=F32)
    h = h + gate * proj
    o_ref[...] = _rmsnorm(h, fnw_ref[...]) if last_layer else h


def _ple_final(h, p, ple_norm, w_gate, w_proj, final_norm, *, tm, last_layer):
    t = h.shape[0]
    ple_dim = p.shape[1]
    return pl.pallas_call(
        functools.partial(_ple_final_body, last_layer=last_layer),
        grid=(t // tm,),
        in_specs=[
            pl.BlockSpec((tm, D_MODEL), lambda i: (i, 0)),
            pl.BlockSpec((tm, ple_dim), lambda i: (i, 0)),
            pl.BlockSpec((1, D_MODEL), lambda i: (0, 0)),
            pl.BlockSpec((D_MODEL, D_MODEL), lambda i: (0, 0)),
            pl.BlockSpec((ple_dim, D_MODEL), lambda i: (0, 0)),
            pl.BlockSpec((1, D_MODEL), lambda i: (0, 0)),
        ],
        out_specs=pl.BlockSpec((tm, D_MODEL), lambda i: (i, 0)),
        out_shape=jax.ShapeDtypeStruct((t, D_MODEL), F32),
        compiler_params=_params(1),
        name="ple_final",
    )(h, p, ple_norm, w_gate, w_proj, final_norm)


def _head_selector(width, n_terms):
    src = jnp.arange(LANES, dtype=jnp.int32)[:, None]
    dst = jnp.arange(SSD_HEADS * width, dtype=jnp.int32)[None, :] // width
    return jnp.tile((src == dst).astype(BF16), (n_terms, 1))


def _pad_lanes(v):
    return jnp.pad(v.astype(F32), (0, LANES - v.shape[0]))[None, :]


def kernel(x, p, ffn1_norm, ffn1_w_in, ffn1_w_out, mix_norm, mix_w_in, sc_conv_w, ssd_conv_w,
           ssd_conv_b, ssd_dt_bias, ssd_a_log, ssd_d, ssd_norm, mix_w_out, ffn2_norm, ffn2_w_in,
           ffn2_w_out, ple_norm, ple_w_gate, ple_w_proj, final_norm):
    batch, seq, _ = x.shape
    depth = ffn1_norm.shape[0]
    t = batch * seq
    h = x.reshape(t, D_MODEL)
    e64 = _head_selector(SSD_HEAD_DIM, SCALE_SPLIT)
    row = lambda v: v.astype(F32)[None, :]

    for i in range(depth):
        h = _ffn(h, row(ffn1_norm[i]), ffn1_w_in[i], ffn1_w_out[i], tm=1024, tf=512, name="ffn1")

        w_in_t = jnp.swapaxes(mix_w_in, 1, 2)[i]
        w_in = _transpose_cast(w_in_t, n_rows=MAIN_COLS, bn=512)
        w_dt = _transpose_cast(jnp.pad(w_in_t[MAIN_COLS:], ((0, LANES - SSD_HEADS), (0, 0))),
                               n_rows=LANES, bn=LANES)
        w_out = mix_w_out[i].astype(BF16)
        cw = ssd_conv_w[i].astype(F32)
        cb = ssd_conv_b[i].astype(F32)[None, :]
        h_ssd = _ssd_mix(h, row(mix_norm[i]), w_in, w_dt,
                         cw[:, :SSD_WIDTH], cw[:, SSD_WIDTH:], cb[:, :SSD_WIDTH], cb[:, SSD_WIDTH:],
                         _pad_lanes(ssd_dt_bias[i]), _pad_lanes(ssd_a_log[i]),
                         jnp.repeat(ssd_d[i].astype(F32), SSD_HEAD_DIM)[None, :],
                         row(ssd_norm[i]), e64, w_out, seq=seq)
        h = _conv_mix(h, h_ssd, row(mix_norm[i]), w_in, sc_conv_w[i].astype(F32), w_out, seq=seq)

        h = _ffn(h, row(ffn2_norm[i]), ffn2_w_in[i], ffn2_w_out[i], tm=1024, tf=512, name="ffn2")

        h = _ple_final(h, p[i].reshape(t, -1), row(ple_norm[i]), ple_w_gate[i].astype(BF16),
                       ple_w_proj[i].astype(BF16), row(final_norm), tm=512,
                       last_layer=(i + 1 == depth))
    return h.reshape(batch, seq, D_MODEL)
```

```python
import functools

import jax
import jax.numpy as jnp
from jax import lax
from jax.experimental import pallas as pl
from jax.experimental.pallas import tpu as pltpu

F32 = jnp.float32
BF16 = jnp.bfloat16

D_MODEL = 2048
D_FF = 5632
CONV_WIDTH = 2048
SC_KERNEL = 3
SSD_HEADS = 32
SSD_HEAD_DIM = 64
SSD_WIDTH = SSD_HEADS * SSD_HEAD_DIM
SSD_STATE = 128
SSD_GROUPS = 4
SSD_CONV = 4
CHUNK = 128
SSD_BC = SSD_GROUPS * SSD_STATE
SSD_XBC = SSD_WIDTH + 2 * SSD_BC
EPS = 1e-6

LANES = 128
SUBLANES = 8
VMEM_LIMIT_BYTES = 60000 * 1024
FFN_VMEM_LIMIT_BYTES = 62 * 1024 * 1024
GROUP_W = SSD_WIDTH // SSD_GROUPS
HEADS_PER_GROUP = SSD_HEADS // SSD_GROUPS
QUAD = 4
QUAD_W = QUAD * SSD_HEAD_DIM
LOG_SPLIT = 3
SCALE_SPLIT = 2

CONV_IN_COLS = 3 * CONV_WIDTH
MAIN_COLS = CONV_IN_COLS + SSD_WIDTH + SSD_XBC
STEP_ROWS = 2 * CHUNK


def _params(n_axes, vmem_limit_bytes=VMEM_LIMIT_BYTES):
    return pltpu.CompilerParams(
        dimension_semantics=("arbitrary",) * n_axes,
        vmem_limit_bytes=vmem_limit_bytes)


def _rmsnorm(x, w):
    return x * lax.rsqrt(jnp.mean(x * x, axis=-1, keepdims=True) + EPS) * w


def _silu(x):
    return x * jax.nn.sigmoid(x)


def _resident(shape, block_index=None):
    index = (0,) * len(shape) if block_index is None else block_index
    return pl.BlockSpec(shape, lambda *_: index, pipeline_mode=pl.Buffered(1))


def _ffn_body(x_hbm, nw_ref, wg_ref, wu_ref, wo_ref, o_ref, xn_ref, xbuf, sem, *, tm):
    i = pl.program_id(0)
    j = pl.program_id(1)

    def x_copy(tile):
        return pltpu.make_async_copy(x_hbm.at[pl.ds(tile * tm, tm), :], xbuf, sem)

    @pl.when((i == 0) & (j == 0))
    def _():
        x_copy(0).start()

    @pl.when(j == 0)
    def _():
        x_copy(i).wait()
        nw = nw_ref[...]

        def slab(r, carry):
            rows = pl.ds(pl.multiple_of(r * CHUNK, CHUNK), CHUNK)
            x = xbuf[rows, :]
            xn_ref[rows, :] = _rmsnorm(x, nw).astype(BF16)
            o_ref[rows, :] = 2.0 * x
            return carry

        lax.fori_loop(0, tm // CHUNK, slab, None)

    @pl.when((j == 1) & (i + 1 < pl.num_programs(0)))
    def _():
        x_copy(i + 1).start()

    xn = xn_ref[...]
    g = jnp.dot(xn, wg_ref[...].astype(BF16), preferred_element_type=F32)
    u = jnp.dot(xn, wu_ref[...].astype(BF16), preferred_element_type=F32)
    a = (_silu(g) * u).astype(BF16)
    o_ref[...] += jnp.dot(a, wo_ref[...].astype(BF16), preferred_element_type=F32)

    @pl.when(j == pl.num_programs(1) - 1)
    def _():
        o_ref[...] = 0.5 * o_ref[...]


def _ffn(h, norm_w, w_in, w_out, *, tm, tf, name):
    t = h.shape[0]
    nj = D_FF // tf
    assert nj >= 2, "the x prefetch is issued in the second d_ff step"
    return pl.pallas_call(
        functools.partial(_ffn_body, tm=tm),
        grid=(t // tm, nj),
        in_specs=[
            pl.BlockSpec(memory_space=pl.ANY),
            pl.BlockSpec((1, D_MODEL), lambda i, j: (0, 0)),
            pl.BlockSpec((D_MODEL, tf), lambda i, j: (0, j)),
            pl.BlockSpec((D_MODEL, tf), lambda i, j: (0, j + nj)),
            pl.BlockSpec((tf, D_MODEL), lambda i, j: (j, 0)),
        ],
        out_specs=pl.BlockSpec((tm, D_MODEL), lambda i, j: (i, 0)),
        out_shape=jax.ShapeDtypeStruct((t, D_MODEL), F32),
        scratch_shapes=[
            pltpu.VMEM((tm, D_MODEL), BF16),
            pltpu.VMEM((tm, D_MODEL), F32),
            pltpu.SemaphoreType.DMA(()),
        ],
        compiler_params=_params(2, FFN_VMEM_LIMIT_BYTES),
        name=name,
    )(h, norm_w, w_in, w_in, w_out)


def _causal_conv(buf_ref, w_ref, n_taps):
    ext = buf_ref[...]
    acc = None
    for k in range(n_taps):
        delay = n_taps - 1 - k
        shifted = ext if delay == 0 else pltpu.roll(ext, delay, 0)
        tap = shifted[SUBLANES:, :] * w_ref[k:k + 1, :]
        acc = tap if acc is None else acc + tap
    return acc


_BODY = pl.ds(SUBLANES, CHUNK)
_TAIL = pl.ds(CHUNK, SUBLANES)
_HEAD = pl.ds(0, SUBLANES)


def _proj(xn, w_ref):
    return jnp.dot(xn, w_ref[...], preferred_element_type=F32)


def _step_rows():
    return pl.BlockSpec((STEP_ROWS, D_MODEL), lambda s: (s, 0))


def _conv_mix_body(x_ref, res_ref, nw_ref, wb_ref, wc_ref, wx_ref, scw_ref, wout_ref, o_ref,
                   pbuf, *, steps_per_seq):
    @pl.when(pl.program_id(0) % steps_per_seq == 0)
    def _():
        pbuf[_HEAD, :] = jnp.zeros((SUBLANES, CONV_WIDTH), F32)

    for r in range(0, STEP_ROWS, CHUNK):
        rows = pl.ds(r, CHUNK)
        xn = _rmsnorm(x_ref[rows, :], nw_ref[...]).astype(BF16)
        pbuf[_BODY, :] = _proj(xn, wc_ref) * _proj(xn, wx_ref)
        y_a = _proj(xn, wb_ref) * _causal_conv(pbuf, scw_ref, SC_KERNEL)
        pbuf[_HEAD, :] = pbuf[_TAIL, :]
        o_ref[rows, :] = res_ref[rows, :] + jnp.dot(
            y_a.astype(BF16), wout_ref[...], preferred_element_type=F32)


def _conv_mix(h, h_res, norm_w, w_in, sc_w, w_out, *, seq):
    t = h.shape[0]
    return pl.pallas_call(
        functools.partial(_conv_mix_body, steps_per_seq=seq // STEP_ROWS),
        grid=(t // STEP_ROWS,),
        in_specs=[
            _step_rows(), _step_rows(),
            _resident((1, D_MODEL)),
            _resident((D_MODEL, CONV_WIDTH), (0, 0)),
            _resident((D_MODEL, CONV_WIDTH), (0, 1)),
            _resident((D_MODEL, CONV_WIDTH), (0, 2)),
            _resident((SC_KERNEL, CONV_WIDTH)),
            _resident((CONV_WIDTH, D_MODEL), (0, 0)),
        ],
        out_specs=pl.BlockSpec((STEP_ROWS, D_MODEL), lambda s: (s, 0)),
        out_shape=jax.ShapeDtypeStruct((t, D_MODEL), F32),
        scratch_shapes=[pltpu.VMEM((CHUNK + SUBLANES, CONV_WIDTH), F32)],
        compiler_params=_params(1),
        name="conv_mix",
    )(h, h_res, norm_w, w_in, w_in, w_in, sc_w, w_out)


def _split_cat(a, axis, n_terms):
    terms = []
    rest = a
    for _ in range(n_terms):
        term = rest.astype(BF16)
        terms.append(term)
        rest = rest - term.astype(F32)
    return jnp.concatenate(terms, axis=axis)


def _ssd_chunk(xn, wz_ref, wx_ref, wbc_ref, wdt_ref, cwx_ref, cwbc_ref, cbx_ref, cbbc_ref,
               dtb_ref, alog_ref, dsk_ref, gnw_ref, e64_ref, xbuf, bcbuf, st_ref):
    xbuf[_BODY, :] = _proj(xn, wx_ref)
    xc = _silu(_causal_conv(xbuf, cwx_ref, SSD_CONV) + cbx_ref[...])
    xbuf[_HEAD, :] = xbuf[_TAIL, :]
    bcbuf[_BODY, :] = _proj(xn, wbc_ref)
    bcc = _silu(_causal_conv(bcbuf, cwbc_ref, SSD_CONV) + cbbc_ref[...])
    bcbuf[_HEAD, :] = bcbuf[_TAIL, :]

    dt_in = _proj(xn, wdt_ref) + dtb_ref[...]
    dtv = jnp.maximum(dt_in, 0.0) + jnp.log1p(jnp.exp(-jnp.abs(dt_in)))
    a_neg = -jnp.exp(alog_ref[...])
    row = lax.broadcasted_iota(jnp.int32, (CHUNK, CHUNK), 0)
    col = lax.broadcasted_iota(jnp.int32, (CHUNK, CHUNK), 1)
    causal = col <= row
    tril = causal.astype(BF16)
    acs = jnp.dot(jnp.concatenate([tril] * LOG_SPLIT, axis=1),
                  _split_cat(dtv * a_neg, 0, LOG_SPLIT),
                  preferred_element_type=F32)
    acs_t = acs.T
    decay = jnp.exp(acs[CHUNK - 1:CHUNK, :] - acs)
    exp_acs = jnp.exp(acs)

    per_head = jnp.concatenate([dtv, dtv * decay, exp_acs], axis=0)
    per_chan = jnp.dot(_split_cat(per_head, 1, SCALE_SPLIT), e64_ref[...],
                       preferred_element_type=F32)
    dt_e = per_chan[0:CHUNK]
    dt_dec_e = per_chan[CHUNK:2 * CHUNK]
    exp_acs_e = per_chan[2 * CHUNK:3 * CHUNK]
    chunk_decay = exp_acs_e[CHUNK - 1:CHUNK, :]

    x_dt_b = (xc * dt_e).astype(BF16)
    x_dec_b = (xc * dt_dec_e).astype(BF16)

    lane_head = lax.broadcasted_iota(jnp.int32, (CHUNK, QUAD_W), 1) // SSD_HEAD_DIM
    y_groups = []
    for g in range(SSD_GROUPS):
        gsl = slice(g * GROUP_W, (g + 1) * GROUP_W)
        b_g = bcc[:, g * SSD_STATE:(g + 1) * SSD_STATE]
        c_g = bcc[:, SSD_BC + g * SSD_STATE:SSD_BC + (g + 1) * SSD_STATE]
        b_gb = b_g.astype(BF16)
        c_gb = c_g.astype(BF16)
        cb = lax.dot_general(c_gb, b_gb, (((1,), (1,)), ((), ())),
                             preferred_element_type=F32)
        cb = jnp.where(causal, cb, 0.0)
        states = jnp.dot(b_g.T.astype(BF16), x_dec_b[:, gsl], preferred_element_type=F32)
        prev = st_ref[:, gsl]
        y_off = jnp.dot(c_gb, prev.astype(BF16), preferred_element_type=F32) * exp_acs_e[:, gsl]
        st_ref[:, gsl] = prev * chunk_decay[:, gsl] + states
        y_diag = []
        for q in range(HEADS_PER_GROUP // QUAD):
            h0 = g * HEADS_PER_GROUP + q * QUAD
            m_parts = []
            x_parts = []
            x_q = x_dt_b[:, h0 * SSD_HEAD_DIM:h0 * SSD_HEAD_DIM + QUAD_W]
            for hh in range(QUAD):
                h = h0 + hh
                diff = jnp.broadcast_to(acs[:, h:h + 1], (CHUNK, CHUNK)) - acs_t[h:h + 1, :]
                m_parts.append((cb * jnp.exp(jnp.where(causal, diff, 0.0))).astype(BF16))
                x_parts.append(jnp.where(lane_head == hh, x_q, jnp.zeros_like(x_q)))
            m_q = jnp.concatenate(m_parts, axis=1)
            x_bd = jnp.concatenate(x_parts, axis=0)
            y_diag.append(jnp.dot(m_q, x_bd, preferred_element_type=F32))
        y_groups.append(jnp.concatenate(y_diag, axis=1) + y_off)
    y = jnp.concatenate(y_groups, axis=1) + dsk_ref[...] * xc

    gz = y * _silu(_proj(xn, wz_ref))
    gnw = gnw_ref[...]
    out = []
    for g in range(SSD_GROUPS):
        gsl = slice(g * GROUP_W, (g + 1) * GROUP_W)
        gg = gz[:, gsl]
        out.append(gg * lax.rsqrt(jnp.mean(gg * gg, axis=-1, keepdims=True) + EPS) * gnw[:, gsl])
    return jnp.concatenate(out, axis=1)


def _ssd_mix_body(x_ref, nw_ref, wz_ref, wx_ref, wbc_ref, wdt_ref,
                  cwx_ref, cwbc_ref, cbx_ref, cbbc_ref,
                  dtb_ref, alog_ref, dsk_ref, gnw_ref, e64_ref, wout_ref, o_ref,
                  xbuf, bcbuf, st_ref, *, steps_per_seq):
    @pl.when(pl.program_id(0) % steps_per_seq == 0)
    def _():
        xbuf[_HEAD, :] = jnp.zeros((SUBLANES, SSD_WIDTH), F32)
        bcbuf[_HEAD, :] = jnp.zeros((SUBLANES, 2 * SSD_BC), F32)
        st_ref[...] = jnp.zeros_like(st_ref)

    for r in range(0, STEP_ROWS, CHUNK):
        rows = pl.ds(r, CHUNK)
        xn = _rmsnorm(x_ref[rows, :], nw_ref[...]).astype(BF16)
        y_b = _ssd_chunk(xn, wz_ref, wx_ref, wbc_ref, wdt_ref, cwx_ref, cwbc_ref, cbx_ref,
                         cbbc_ref, dtb_ref, alog_ref, dsk_ref, gnw_ref, e64_ref,
                         xbuf, bcbuf, st_ref)
        o_ref[rows, :] = x_ref[rows, :] + jnp.dot(
            y_b.astype(BF16), wout_ref[...], preferred_element_type=F32)


def _ssd_mix(h, norm_w, w_in, w_dt, cw_x, cw_bc, cb_x, cb_bc, dt_bias, a_log, d_skip,
             gnorm_w, e64, w_out, *, seq):
    t = h.shape[0]
    return pl.pallas_call(
        functools.partial(_ssd_mix_body, steps_per_seq=seq // STEP_ROWS),
        grid=(t // STEP_ROWS,),
        in_specs=[
            _step_rows(),
            _resident((1, D_MODEL)),
            _resident((D_MODEL, SSD_WIDTH), (0, CONV_IN_COLS // SSD_WIDTH)),
            _resident((D_MODEL, SSD_WIDTH), (0, CONV_IN_COLS // SSD_WIDTH + 1)),
            _resident((D_MODEL, 2 * SSD_BC), (0, (CONV_IN_COLS + 2 * SSD_WIDTH) // (2 * SSD_BC))),
            _resident((D_MODEL, LANES)),
            _resident((SSD_CONV, SSD_WIDTH)),
            _resident((SSD_CONV, 2 * SSD_BC)),
            _resident((1, SSD_WIDTH)),
            _resident((1, 2 * SSD_BC)),
            _resident((1, LANES)),
            _resident((1, LANES)),
            _resident((1, SSD_WIDTH)),
            _resident((1, SSD_WIDTH)),
            _resident((SCALE_SPLIT * LANES, SSD_WIDTH)),
            _resident((SSD_WIDTH, D_MODEL), (1, 0)),
        ],
        out_specs=pl.BlockSpec((STEP_ROWS, D_MODEL), lambda s: (s, 0)),
        out_shape=jax.ShapeDtypeStruct((t, D_MODEL), F32),
        scratch_shapes=[
            pltpu.VMEM((CHUNK + SUBLANES, SSD_WIDTH), F32),
            pltpu.VMEM((CHUNK + SUBLANES, 2 * SSD_BC), F32),
            pltpu.VMEM((SSD_STATE, SSD_WIDTH), F32),
        ],
        compiler_params=_params(1),
        name="ssd_mix",
    )(h, norm_w, w_in, w_in, w_in, w_dt, cw_x, cw_bc, cb_x, cb_bc, dt_bias, a_log,
      d_skip, gnorm_w, e64, w_out)


def _transpose_cast_body(w_ref, o_ref):
    o_ref[...] = w_ref[...].T.astype(BF16)


def _transpose_cast(w_t, *, n_rows, bn):
    d = w_t.shape[1]
    return pl.pallas_call(
        _transpose_cast_body,
        grid=(n_rows // bn,),
        in_specs=[pl.BlockSpec((bn, d), lambda j: (j, 0))],
        out_specs=pl.BlockSpec((d, bn), lambda j: (0, j)),
        out_shape=jax.ShapeDtypeStruct((d, n_rows), BF16),
        compiler_params=_params(1),
        name="transpose_cast",
    )(w_t)


def _ple_final_body(h_ref, p_ref, pnw_ref, wg_ref, wp_ref, fnw_ref, o_ref, *, last_layer):
    h = h_ref[...]
    xn = _rmsnorm(h, pnw_ref[...]).astype(BF16)
    gate = jax.nn.sigmoid(jnp.dot(xn, wg_ref[...], preferred_element_type=F32))
    proj = jnp.dot(p_ref[...].astype(BF16), wp_ref[...], preferred_element_type=F32)
    h = h + gate * proj
    o_ref[...] = _rmsnorm(h, fnw_ref[...]) if last_layer else h


def _ple_final(h, p, ple_norm, w_gate, w_proj, final_norm, *, tm, last_layer):
    t = h.shape[0]
    ple_dim = p.shape[1]
    return pl.pallas_call(
        functools.partial(_ple_final_body, last_layer=last_layer),
        grid=(t // tm,),
        in_specs=[
            pl.BlockSpec((tm, D_MODEL), lambda i: (i, 0)),
            pl.BlockSpec((tm, ple_dim), lambda i: (i, 0)),
            pl.BlockSpec((1, D_MODEL), lambda i: (0, 0)),
            pl.BlockSpec((D_MODEL, D_MODEL), lambda i: (0, 0)),
            pl.BlockSpec((ple_dim, D_MODEL), lambda i: (0, 0)),
            pl.BlockSpec((1, D_MODEL), lambda i: (0, 0)),
        ],
        out_specs=pl.BlockSpec((tm, D_MODEL), lambda i: (i, 0)),
        out_shape=jax.ShapeDtypeStruct((t, D_MODEL), F32),
        compiler_params=_params(1),
        name="ple_final",
    )(h, p, ple_norm, w_gate, w_proj, final_norm)


def _head_selector(width, n_terms):
    src = jnp.arange(LANES, dtype=jnp.int32)[:, None]
    dst = jnp.arange(SSD_HEADS * width, dtype=jnp.int32)[None, :] // width
    return jnp.tile((src == dst).astype(BF16), (n_terms, 1))


def _pad_lanes(v):
    return jnp.pad(v.astype(F32), (0, LANES - v.shape[0]))[None, :]


def kernel(x, p, ffn1_norm, ffn1_w_in, ffn1_w_out, mix_norm, mix_w_in, sc_conv_w, ssd_conv_w,
           ssd_conv_b, ssd_dt_bias, ssd_a_log, ssd_d, ssd_norm, mix_w_out, ffn2_norm, ffn2_w_in,
           ffn2_w_out, ple_norm, ple_w_gate, ple_w_proj, final_norm):
    batch, seq, _ = x.shape
    depth = ffn1_norm.shape[0]
    t = batch * seq
    h = x.reshape(t, D_MODEL)
    e64 = _head_selector(SSD_HEAD_DIM, SCALE_SPLIT)
    row = lambda v: v.astype(F32)[None, :]

    for i in range(depth):
        h = _ffn(h, row(ffn1_norm[i]), ffn1_w_in[i], ffn1_w_out[i], tm=1024, tf=512, name="ffn1")

        w_in_t = jnp.swapaxes(mix_w_in, 1, 2)[i]
        w_in = _transpose_cast(w_in_t, n_rows=MAIN_COLS, bn=512)
        w_dt = _transpose_cast(jnp.pad(w_in_t[MAIN_COLS:], ((0, LANES - SSD_HEADS), (0, 0))),
                               n_rows=LANES, bn=LANES)
        w_out = mix_w_out[i].astype(BF16)
        cw = ssd_conv_w[i].astype(F32)
        cb = ssd_conv_b[i].astype(F32)[None, :]
        h_ssd = _ssd_mix(h, row(mix_norm[i]), w_in, w_dt,
                         cw[:, :SSD_WIDTH], cw[:, SSD_WIDTH:], cb[:, :SSD_WIDTH], cb[:, SSD_WIDTH:],
                         _pad_lanes(ssd_dt_bias[i]), _pad_lanes(ssd_a_log[i]),
                         jnp.repeat(ssd_d[i].astype(F32), SSD_HEAD_DIM)[None, :],
                         row(ssd_norm[i]), e64, w_out, seq=seq)
        h = _conv_mix(h, h_ssd, row(mix_norm[i]), w_in, sc_conv_w[i].astype(F32), w_out, seq=seq)

        h = _ffn(h, row(ffn2_norm[i]), ffn2_w_in[i], ffn2_w_out[i], tm=1024, tf=512, name="ffn2")

        h = _ple_final(h, p[i].reshape(t, -1), row(ple_norm[i]), ple_w_gate[i].astype(BF16),
                       ple_w_proj[i].astype(BF16), row(final_norm), tm=512,
                       last_layer=(i + 1 == depth))
    return h.reshape(batch, seq, D_MODEL)
```

```python
import functools

import jax
import jax.numpy as jnp
from jax import lax
from jax.experimental import pallas as pl
from jax.experimental.pallas import tpu as pltpu

F32 = jnp.float32
BF16 = jnp.bfloat16

D_MODEL = 2048
D_FF = 5632
CONV_WIDTH = 2048
SC_KERNEL = 3
SSD_HEADS = 32
SSD_HEAD_DIM = 64
SSD_WIDTH = SSD_HEADS * SSD_HEAD_DIM
SSD_STATE = 128
SSD_GROUPS = 4
SSD_CONV = 4
CHUNK = 128
SSD_BC = SSD_GROUPS * SSD_STATE
SSD_XBC = SSD_WIDTH + 2 * SSD_BC
EPS = 1e-6

LANES = 128
SUBLANES = 8
VMEM_LIMIT_BYTES = 60000 * 1024
FFN_VMEM_LIMIT_BYTES = 63 * 1024 * 1024
GROUP_W = SSD_WIDTH // SSD_GROUPS
HEADS_PER_GROUP = SSD_HEADS // SSD_GROUPS
QUAD = 4
QUAD_W = QUAD * SSD_HEAD_DIM
LOG_SPLIT = 3
SCALE_SPLIT = 2

CONV_IN_COLS = 3 * CONV_WIDTH
MAIN_COLS = CONV_IN_COLS + SSD_WIDTH + SSD_XBC
STEP_ROWS = 2 * CHUNK


def _params(n_axes, vmem_limit_bytes=VMEM_LIMIT_BYTES):
    return pltpu.CompilerParams(
        dimension_semantics=("arbitrary",) * n_axes,
        vmem_limit_bytes=vmem_limit_bytes)


def _rmsnorm(x, w):
    return x * lax.rsqrt(jnp.mean(x * x, axis=-1, keepdims=True) + EPS) * w


def _silu(x):
    return x * jax.nn.sigmoid(x)


def _resident(shape, block_index=None):
    index = (0,) * len(shape) if block_index is None else block_index
    return pl.BlockSpec(shape, lambda *_: index, pipeline_mode=pl.Buffered(1))


def _ffn_body(x_hbm, nw_ref, wg_ref, wu_ref, wo_ref, *rest, tm, with_side_job):
    if with_side_job:
        side_in_ref, o_ref, side_out_ref, xn_ref, xbuf, sem = rest
    else:
        o_ref, xn_ref, xbuf, sem = rest
    i = pl.program_id(0)
    j = pl.program_id(1)

    def x_copy(tile):
        return pltpu.make_async_copy(x_hbm.at[pl.ds(tile * tm, tm), :], xbuf, sem)

    @pl.when((i == 0) & (j == 0))
    def _():
        x_copy(0).start()

    @pl.when(j == 0)
    def _():
        x_copy(i).wait()
        nw = nw_ref[...]

        def slab(r, carry):
            rows = pl.ds(pl.multiple_of(r * CHUNK, CHUNK), CHUNK)
            x = xbuf[rows, :]
            xn_ref[rows, :] = _rmsnorm(x, nw).astype(BF16)
            o_ref[rows, :] = 2.0 * x
            return carry

        lax.fori_loop(0, tm // CHUNK, slab, None)

    @pl.when((j == 1) & (i + 1 < pl.num_programs(0)))
    def _():
        x_copy(i + 1).start()

    if with_side_job:
        side_out_ref[...] = side_in_ref[...].T.astype(BF16)
    xn = xn_ref[...]
    g = jnp.dot(xn, wg_ref[...].astype(BF16), preferred_element_type=F32)
    u = jnp.dot(xn, wu_ref[...].astype(BF16), preferred_element_type=F32)
    a = (_silu(g) * u).astype(BF16)
    o_ref[...] += jnp.dot(a, wo_ref[...].astype(BF16), preferred_element_type=F32)

    @pl.when(j == pl.num_programs(1) - 1)
    def _():
        o_ref[...] = 0.5 * o_ref[...]


SIDE_ROWS = 128


def _ffn(h, norm_w, w_in, w_out, side_w_t=None, *, tm, tf, name):
    t = h.shape[0]
    nj = D_FF // tf
    assert nj >= 2, "the x prefetch is issued in the second d_ff step"
    in_specs = [
        pl.BlockSpec(memory_space=pl.ANY),
        pl.BlockSpec((1, D_MODEL), lambda i, j: (0, 0)),
        pl.BlockSpec((D_MODEL, tf), lambda i, j: (0, j)),
        pl.BlockSpec((D_MODEL, tf), lambda i, j: (0, j + nj)),
        pl.BlockSpec((tf, D_MODEL), lambda i, j: (j, 0)),
    ]
    out_specs = [pl.BlockSpec((tm, D_MODEL), lambda i, j: (i, 0))]
    out_shape = [jax.ShapeDtypeStruct((t, D_MODEL), F32)]
    operands = [h, norm_w, w_in, w_in, w_out]
    if side_w_t is not None:
        side_rows = (t // tm) * nj * SIDE_ROWS
        assert side_rows <= side_w_t.shape[0]
        in_specs.append(pl.BlockSpec((SIDE_ROWS, D_MODEL), lambda i, j: (i * nj + j, 0)))
        out_specs.append(pl.BlockSpec((D_MODEL, SIDE_ROWS), lambda i, j: (0, i * nj + j)))
        out_shape.append(jax.ShapeDtypeStruct((D_MODEL, side_rows), BF16))
        operands.append(side_w_t)
    return pl.pallas_call(
        functools.partial(_ffn_body, tm=tm, with_side_job=side_w_t is not None),
        grid=(t // tm, nj),
        in_specs=in_specs,
        out_specs=out_specs,
        out_shape=out_shape,
        scratch_shapes=[
            pltpu.VMEM((tm, D_MODEL), BF16),
            pltpu.VMEM((tm, D_MODEL), F32),
            pltpu.SemaphoreType.DMA(()),
        ],
        compiler_params=_params(2, FFN_VMEM_LIMIT_BYTES),
        name=name,
    )(*operands)


def _causal_conv(buf_ref, w_ref, n_taps):
    ext = buf_ref[...]
    acc = None
    for k in range(n_taps):
        delay = n_taps - 1 - k
        shifted = ext if delay == 0 else pltpu.roll(ext, delay, 0)
        tap = shifted[SUBLANES:, :] * w_ref[k:k + 1, :]
        acc = tap if acc is None else acc + tap
    return acc


_BODY = pl.ds(SUBLANES, CHUNK)
_TAIL = pl.ds(CHUNK, SUBLANES)
_HEAD = pl.ds(0, SUBLANES)


def _proj(xn, w_ref):
    return jnp.dot(xn, w_ref[...], preferred_element_type=F32)


def _step_rows():
    return pl.BlockSpec((STEP_ROWS, D_MODEL), lambda s: (s, 0))


def _conv_mix_body(x_ref, res_ref, nw_ref, wb_ref, wc_ref, wx_ref, scw_ref, wout_ref, o_ref,
                   pbuf, *, steps_per_seq):
    @pl.when(pl.program_id(0) % steps_per_seq == 0)
    def _():
        pbuf[_HEAD, :] = jnp.zeros((SUBLANES, CONV_WIDTH), F32)

    for r in range(0, STEP_ROWS, CHUNK):
        rows = pl.ds(r, CHUNK)
        xn = _rmsnorm(x_ref[rows, :], nw_ref[...]).astype(BF16)
        pbuf[_BODY, :] = _proj(xn, wc_ref) * _proj(xn, wx_ref)
        y_a = _proj(xn, wb_ref) * _causal_conv(pbuf, scw_ref, SC_KERNEL)
        pbuf[_HEAD, :] = pbuf[_TAIL, :]
        o_ref[rows, :] = res_ref[rows, :] + jnp.dot(
            y_a.astype(BF16), wout_ref[...], preferred_element_type=F32)


def _conv_mix(h, h_res, norm_w, w_in, sc_w, w_out, *, seq):
    t = h.shape[0]
    return pl.pallas_call(
        functools.partial(_conv_mix_body, steps_per_seq=seq // STEP_ROWS),
        grid=(t // STEP_ROWS,),
        in_specs=[
            _step_rows(), _step_rows(),
            _resident((1, D_MODEL)),
            _resident((D_MODEL, CONV_WIDTH), (0, 0)),
            _resident((D_MODEL, CONV_WIDTH), (0, 1)),
            _resident((D_MODEL, CONV_WIDTH), (0, 2)),
            _resident((SC_KERNEL, CONV_WIDTH)),
            _resident((CONV_WIDTH, D_MODEL), (0, 0)),
        ],
        out_specs=pl.BlockSpec((STEP_ROWS, D_MODEL), lambda s: (s, 0)),
        out_shape=jax.ShapeDtypeStruct((t, D_MODEL), F32),
        scratch_shapes=[pltpu.VMEM((CHUNK + SUBLANES, CONV_WIDTH), F32)],
        compiler_params=_params(1),
        name="conv_mix",
    )(h, h_res, norm_w, w_in, w_in, w_in, sc_w, w_out)


def _split_cat(a, axis, n_terms):
    terms = []
    rest = a
    for _ in range(n_terms):
        term = rest.astype(BF16)
        terms.append(term)
        rest = rest - term.astype(F32)
    return jnp.concatenate(terms, axis=axis)


def _ssd_chunk(xn, wz_ref, wx_ref, wbc_ref, wdt_ref, cwx_ref, cwbc_ref, cbx_ref, cbbc_ref,
               dtb_ref, alog_ref, dsk_ref, gnw_ref, e64_ref, xbuf, bcbuf, st_ref):
    xbuf[_BODY, :] = _proj(xn, wx_ref)
    xc = _silu(_causal_conv(xbuf, cwx_ref, SSD_CONV) + cbx_ref[...])
    xbuf[_HEAD, :] = xbuf[_TAIL, :]
    bcbuf[_BODY, :] = _proj(xn, wbc_ref)
    bcc = _silu(_causal_conv(bcbuf, cwbc_ref, SSD_CONV) + cbbc_ref[...])
    bcbuf[_HEAD, :] = bcbuf[_TAIL, :]

    dt_in = _proj(xn, wdt_ref) + dtb_ref[...]
    dtv = jnp.maximum(dt_in, 0.0) + jnp.log1p(jnp.exp(-jnp.abs(dt_in)))
    a_neg = -jnp.exp(alog_ref[...])
    row = lax.broadcasted_iota(jnp.int32, (CHUNK, CHUNK), 0)
    col = lax.broadcasted_iota(jnp.int32, (CHUNK, CHUNK), 1)
    causal = col <= row
    tril = causal.astype(BF16)
    acs = jnp.dot(jnp.concatenate([tril] * LOG_SPLIT, axis=1),
                  _split_cat(dtv * a_neg, 0, LOG_SPLIT),
                  preferred_element_type=F32)
    acs_t = acs.T
    decay = jnp.exp(acs[CHUNK - 1:CHUNK, :] - acs)
    exp_acs = jnp.exp(acs)

    per_head = jnp.concatenate([dtv, dtv * decay, exp_acs], axis=0)
    per_chan = jnp.dot(_split_cat(per_head, 1, SCALE_SPLIT), e64_ref[...],
                       preferred_element_type=F32)
    dt_e = per_chan[0:CHUNK]
    dt_dec_e = per_chan[CHUNK:2 * CHUNK]
    exp_acs_e = per_chan[2 * CHUNK:3 * CHUNK]
    chunk_decay = exp_acs_e[CHUNK - 1:CHUNK, :]

    x_dt_b = (xc * dt_e).astype(BF16)
    x_dec_b = (xc * dt_dec_e).astype(BF16)

    lane_head = lax.broadcasted_iota(jnp.int32, (CHUNK, QUAD_W), 1) // SSD_HEAD_DIM
    y_groups = []
    for g in range(SSD_GROUPS):
        gsl = slice(g * GROUP_W, (g + 1) * GROUP_W)
        b_g = bcc[:, g * SSD_STATE:(g + 1) * SSD_STATE]
        c_g = bcc[:, SSD_BC + g * SSD_STATE:SSD_BC + (g + 1) * SSD_STATE]
        b_gb = b_g.astype(BF16)
        c_gb = c_g.astype(BF16)
        cb = lax.dot_general(c_gb, b_gb, (((1,), (1,)), ((), ())),
                             preferred_element_type=F32)
        cb = jnp.where(causal, cb, 0.0)
        states = jnp.dot(b_g.T.astype(BF16), x_dec_b[:, gsl], preferred_element_type=F32)
        prev = st_ref[:, gsl]
        y_off = jnp.dot(c_gb, prev.astype(BF16), preferred_element_type=F32) * exp_acs_e[:, gsl]
        st_ref[:, gsl] = prev * chunk_decay[:, gsl] + states
        y_diag = []
        for q in range(HEADS_PER_GROUP // QUAD):
            h0 = g * HEADS_PER_GROUP + q * QUAD
            m_parts = []
            x_parts = []
            x_q = x_dt_b[:, h0 * SSD_HEAD_DIM:h0 * SSD_HEAD_DIM + QUAD_W]
            for hh in range(QUAD):
                h = h0 + hh
                diff = jnp.broadcast_to(acs[:, h:h + 1], (CHUNK, CHUNK)) - acs_t[h:h + 1, :]
                m_parts.append((cb * jnp.exp(jnp.where(causal, diff, 0.0))).astype(BF16))
                x_parts.append(jnp.where(lane_head == hh, x_q, jnp.zeros_like(x_q)))
            m_q = jnp.concatenate(m_parts, axis=1)
            x_bd = jnp.concatenate(x_parts, axis=0)
            y_diag.append(jnp.dot(m_q, x_bd, preferred_element_type=F32))
        y_groups.append(jnp.concatenate(y_diag, axis=1) + y_off)
    y = jnp.concatenate(y_groups, axis=1) + dsk_ref[...] * xc

    gz = y * _silu(_proj(xn, wz_ref))
    gnw = gnw_ref[...]
    out = []
    for g in range(SSD_GROUPS):
        gsl = slice(g * GROUP_W, (g + 1) * GROUP_W)
        gg = gz[:, gsl]
        out.append(gg * lax.rsqrt(jnp.mean(gg * gg, axis=-1, keepdims=True) + EPS) * gnw[:, gsl])
    return jnp.concatenate(out, axis=1)


def _ssd_mix_body(x_ref, nw_ref, wz_ref, wx_ref, wbc_ref, wdt_ref,
                  cwx_ref, cwbc_ref, cbx_ref, cbbc_ref,
                  dtb_ref, alog_ref, dsk_ref, gnw_ref, e64_ref, wout_ref, o_ref,
                  xbuf, bcbuf, st_ref, *, steps_per_seq):
    @pl.when(pl.program_id(0) % steps_per_seq == 0)
    def _():
        xbuf[_HEAD, :] = jnp.zeros((SUBLANES, SSD_WIDTH), F32)
        bcbuf[_HEAD, :] = jnp.zeros((SUBLANES, 2 * SSD_BC), F32)
        st_ref[...] = jnp.zeros_like(st_ref)

    for r in range(0, STEP_ROWS, CHUNK):
        rows = pl.ds(r, CHUNK)
        xn = _rmsnorm(x_ref[rows, :], nw_ref[...]).astype(BF16)
        y_b = _ssd_chunk(xn, wz_ref, wx_ref, wbc_ref, wdt_ref, cwx_ref, cwbc_ref, cbx_ref,
                         cbbc_ref, dtb_ref, alog_ref, dsk_ref, gnw_ref, e64_ref,
                         xbuf, bcbuf, st_ref)
        o_ref[rows, :] = x_ref[rows, :] + jnp.dot(
            y_b.astype(BF16), wout_ref[...], preferred_element_type=F32)


def _ssd_mix(h, norm_w, w_in, w_dt, cw_x, cw_bc, cb_x, cb_bc, dt_bias, a_log, d_skip,
             gnorm_w, e64, w_out, *, seq):
    t = h.shape[0]
    return pl.pallas_call(
        functools.partial(_ssd_mix_body, steps_per_seq=seq // STEP_ROWS),
        grid=(t // STEP_ROWS,),
        in_specs=[
            _step_rows(),
            _resident((1, D_MODEL)),
            _resident((D_MODEL, SSD_WIDTH), (0, CONV_IN_COLS // SSD_WIDTH)),
            _resident((D_MODEL, SSD_WIDTH), (0, CONV_IN_COLS // SSD_WIDTH + 1)),
            _resident((D_MODEL, 2 * SSD_BC), (0, (CONV_IN_COLS + 2 * SSD_WIDTH) // (2 * SSD_BC))),
            _resident((D_MODEL, LANES)),
            _resident((SSD_CONV, SSD_WIDTH)),
            _resident((SSD_CONV, 2 * SSD_BC)),
            _resident((1, SSD_WIDTH)),
            _resident((1, 2 * SSD_BC)),
            _resident((1, LANES)),
            _resident((1, LANES)),
            _resident((1, SSD_WIDTH)),
            _resident((1, SSD_WIDTH)),
            _resident((SCALE_SPLIT * LANES, SSD_WIDTH)),
            _resident((SSD_WIDTH, D_MODEL), (1, 0)),
        ],
        out_specs=pl.BlockSpec((STEP_ROWS, D_MODEL), lambda s: (s, 0)),
        out_shape=jax.ShapeDtypeStruct((t, D_MODEL), F32),
        scratch_shapes=[
            pltpu.VMEM((CHUNK + SUBLANES, SSD_WIDTH), F32),
            pltpu.VMEM((CHUNK + SUBLANES, 2 * SSD_BC), F32),
            pltpu.VMEM((SSD_STATE, SSD_WIDTH), F32),
        ],
        compiler_params=_params(1),
        name="ssd_mix",
    )(h, norm_w, w_in, w_in, w_in, w_dt, cw_x, cw_bc, cb_x, cb_bc, dt_bias, a_log,
      d_skip, gnorm_w, e64, w_out)


def _transpose_cast_body(w_ref, o_ref):
    o_ref[...] = w_ref[...].T.astype(BF16)


def _transpose_cast(w_t, *, n_rows, bn):
    d = w_t.shape[1]
    return pl.pallas_call(
        _transpose_cast_body,
        grid=(n_rows // bn,),
        in_specs=[pl.BlockSpec((bn, d), lambda j: (j, 0))],
        out_specs=pl.BlockSpec((d, bn), lambda j: (0, j)),
        out_shape=jax.ShapeDtypeStruct((d, n_rows), BF16),
        compiler_params=_params(1),
        name="transpose_cast",
    )(w_t)


def _ple_final_body(h_ref, p_ref, pnw_ref, wg_ref, wp_ref, fnw_ref, o_ref, *, last_layer):
    h = h_ref[...]
    xn = _rmsnorm(h, pnw_ref[...]).astype(BF16)
    gate = jax.nn.sigmoid(jnp.dot(xn, wg_ref[...], preferred_element_type=F32))
    proj = jnp.dot(p_ref[...].astype(BF16), wp_ref[...], preferred_element_type=F32)
    h = h + gate * proj
    o_ref[...] = _rmsnorm(h, fnw_ref[...]) if last_layer else h


def _ple_final(h, p, ple_norm, w_gate, w_proj, final_norm, *, tm, last_layer):
    t = h.shape[0]
    ple_dim = p.shape[1]
    return pl.pallas_call(
        functools.partial(_ple_final_body, last_layer=last_layer),
        grid=(t // tm,),
        in_specs=[
            pl.BlockSpec((tm, D_MODEL), lambda i: (i, 0)),
            pl.BlockSpec((tm, ple_dim), lambda i: (i, 0)),
            pl.BlockSpec((1, D_MODEL), lambda i: (0, 0)),
            pl.BlockSpec((D_MODEL, D_MODEL), lambda i: (0, 0)),
            pl.BlockSpec((ple_dim, D_MODEL), lambda i: (0, 0)),
            pl.BlockSpec((1, D_MODEL), lambda i: (0, 0)),
        ],
        out_specs=pl.BlockSpec((tm, D_MODEL), lambda i: (i, 0)),
        out_shape=jax.ShapeDtypeStruct((t, D_MODEL), F32),
        compiler_params=_params(1),
        name="ple_final",
    )(h, p, ple_norm, w_gate, w_proj, final_norm)


def _head_selector(width, n_terms):
    src = jnp.arange(LANES, dtype=jnp.int32)[:, None]
    dst = jnp.arange(SSD_HEADS * width, dtype=jnp.int32)[None, :] // width
    return jnp.tile((src == dst).astype(BF16), (n_terms, 1))


def _pad_lanes(v):
    return jnp.pad(v.astype(F32), (0, LANES - v.shape[0]))[None, :]


def kernel(x, p, ffn1_norm, ffn1_w_in, ffn1_w_out, mix_norm, mix_w_in, sc_conv_w, ssd_conv_w,
           ssd_conv_b, ssd_dt_bias, ssd_a_log, ssd_d, ssd_norm, mix_w_out, ffn2_norm, ffn2_w_in,
           ffn2_w_out, ple_norm, ple_w_gate, ple_w_proj, final_norm):
    batch, seq, _ = x.shape
    depth = ffn1_norm.shape[0]
    t = batch * seq
    h = x.reshape(t, D_MODEL)
    e64 = _head_selector(SSD_HEAD_DIM, SCALE_SPLIT)
    row = lambda v: v.astype(F32)[None, :]

    for i in range(depth):
        w_in_t = jnp.swapaxes(mix_w_in, 1, 2)[i]
        h, w_in = _ffn(h, row(ffn1_norm[i]), ffn1_w_in[i], ffn1_w_out[i], w_in_t,
                       tm=1024, tf=512, name="ffn1")
        assert w_in.shape[1] == MAIN_COLS
        w_dt = _transpose_cast(jnp.pad(w_in_t[MAIN_COLS:], ((0, LANES - SSD_HEADS), (0, 0))),
                               n_rows=LANES, bn=LANES)
        w_out = mix_w_out[i].astype(BF16)
        cw = ssd_conv_w[i].astype(F32)
        cb = ssd_conv_b[i].astype(F32)[None, :]
        h_ssd = _ssd_mix(h, row(mix_norm[i]), w_in, w_dt,
                         cw[:, :SSD_WIDTH], cw[:, SSD_WIDTH:], cb[:, :SSD_WIDTH], cb[:, SSD_WIDTH:],
                         _pad_lanes(ssd_dt_bias[i]), _pad_lanes(ssd_a_log[i]),
                         jnp.repeat(ssd_d[i].astype(F32), SSD_HEAD_DIM)[None, :],
                         row(ssd_norm[i]), e64, w_out, seq=seq)
        h = _conv_mix(h, h_ssd, row(mix_norm[i]), w_in, sc_conv_w[i].astype(F32), w_out, seq=seq)

        h, = _ffn(h, row(ffn2_norm[i]), ffn2_w_in[i], ffn2_w_out[i], tm=1024, tf=512, name="ffn2")

        h = _ple_final(h, p[i].reshape(t, -1), row(ple_norm[i]), ple_w_gate[i].astype(BF16),
                       ple_w_proj[i].astype(BF16), row(final_norm), tm=512,
                       last_layer=(i + 1 == depth))
    return h.reshape(batch, seq, D_MODEL)
```

```python
import functools

import jax
import jax.numpy as jnp
from jax import lax
from jax.experimental import pallas as pl
from jax.experimental.pallas import tpu as pltpu

F32 = jnp.float32
BF16 = jnp.bfloat16

D_MODEL = 2048
D_FF = 5632
CONV_WIDTH = 2048
SC_KERNEL = 3
SSD_HEADS = 32
SSD_HEAD_DIM = 64
SSD_WIDTH = SSD_HEADS * SSD_HEAD_DIM
SSD_STATE = 128
SSD_GROUPS = 4
SSD_CONV = 4
CHUNK = 128
SSD_BC = SSD_GROUPS * SSD_STATE
SSD_XBC = SSD_WIDTH + 2 * SSD_BC
EPS = 1e-6

LANES = 128
SUBLANES = 8
VMEM_LIMIT_BYTES = 60000 * 1024
FFN_VMEM_LIMIT_BYTES = 127 * 512 * 1024
GROUP_W = SSD_WIDTH // SSD_GROUPS
HEADS_PER_GROUP = SSD_HEADS // SSD_GROUPS
QUAD = 4
QUAD_W = QUAD * SSD_HEAD_DIM
LOG_SPLIT = 3
SCALE_SPLIT = 2

CONV_IN_COLS = 3 * CONV_WIDTH
MAIN_COLS = CONV_IN_COLS + SSD_WIDTH + SSD_XBC
STEP_ROWS = 2 * CHUNK


def _params(n_axes, vmem_limit_bytes=VMEM_LIMIT_BYTES):
    return pltpu.CompilerParams(
        dimension_semantics=("arbitrary",) * n_axes,
        vmem_limit_bytes=vmem_limit_bytes)


def _rmsnorm(x, w):
    return x * lax.rsqrt(jnp.mean(x * x, axis=-1, keepdims=True) + EPS) * w


def _silu(x):
    return x * jax.nn.sigmoid(x)


def _resident(shape, block_index=None):
    index = (0,) * len(shape) if block_index is None else block_index
    return pl.BlockSpec(shape, lambda *_: index, pipeline_mode=pl.Buffered(1))


def _ffn_body(x_hbm, nw_ref, wg_ref, wu_ref, wo_ref, *rest, tm, n_transpose, n_cast):
    n_side = n_transpose + n_cast
    side_in = rest[:n_side]
    o_ref = rest[n_side]
    side_out = rest[n_side + 1:2 * n_side + 1]
    xn_ref, xbuf, sem = rest[2 * n_side + 1:]
    i = pl.program_id(0)
    j = pl.program_id(1)

    def x_copy(tile):
        return pltpu.make_async_copy(x_hbm.at[pl.ds(tile * tm, tm), :], xbuf, sem)

    @pl.when((i == 0) & (j == 0))
    def _():
        x_copy(0).start()

    @pl.when(j == 0)
    def _():
        x_copy(i).wait()
        nw = nw_ref[...]

        def slab(r, carry):
            rows = pl.ds(pl.multiple_of(r * CHUNK, CHUNK), CHUNK)
            x = xbuf[rows, :]
            xn_ref[rows, :] = _rmsnorm(x, nw).astype(BF16)
            o_ref[rows, :] = 2.0 * x
            return carry

        lax.fori_loop(0, tm // CHUNK, slab, None)

    @pl.when((j == 1) & (i + 1 < pl.num_programs(0)))
    def _():
        x_copy(i + 1).start()

    for k in range(n_side):
        block = side_in[k][...]
        side_out[k][...] = (block.T if k < n_transpose else block).astype(BF16)
    xn = xn_ref[...]
    g = jnp.dot(xn, wg_ref[...].astype(BF16), preferred_element_type=F32)
    u = jnp.dot(xn, wu_ref[...].astype(BF16), preferred_element_type=F32)
    a = (_silu(g) * u).astype(BF16)
    o_ref[...] += jnp.dot(a, wo_ref[...].astype(BF16), preferred_element_type=F32)

    @pl.when(j == pl.num_programs(1) - 1)
    def _():
        o_ref[...] = 0.5 * o_ref[...]


SIDE_ROWS = 128


def _ffn(h, norm_w, w_in, w_out, *, tm, tf, name, transpose_jobs=(), cast_jobs=()):
    t = h.shape[0]
    nj = D_FF // tf
    assert nj >= 2, "the x prefetch is issued in the second d_ff step"
    in_specs = [
        pl.BlockSpec(memory_space=pl.ANY),
        pl.BlockSpec((1, D_MODEL), lambda i, j: (0, 0)),
        pl.BlockSpec((D_MODEL, tf), lambda i, j: (0, j)),
        pl.BlockSpec((D_MODEL, tf), lambda i, j: (0, j + nj)),
        pl.BlockSpec((tf, D_MODEL), lambda i, j: (j, 0)),
    ]
    out_specs = [pl.BlockSpec((tm, D_MODEL), lambda i, j: (i, 0))]
    out_shape = [jax.ShapeDtypeStruct((t, D_MODEL), F32)]
    operands = [h, norm_w, w_in, w_in, w_out]
    n_steps = (t // tm) * nj
    for w_t in transpose_jobs:
        assert n_steps * SIDE_ROWS <= w_t.shape[0]
        in_specs.append(pl.BlockSpec((SIDE_ROWS, D_MODEL), lambda i, j: (i * nj + j, 0)))
        out_specs.append(pl.BlockSpec((D_MODEL, SIDE_ROWS), lambda i, j: (0, i * nj + j)))
        out_shape.append(jax.ShapeDtypeStruct((D_MODEL, n_steps * SIDE_ROWS), BF16))
        operands.append(w_t)
    for w, rows in cast_jobs:
        last = w.shape[0] // rows - 1
        assert w.shape[0] % rows == 0 and last < n_steps
        spec = pl.BlockSpec((rows, w.shape[1]),
                            lambda i, j, last=last: (jnp.minimum(i * nj + j, last), 0))
        in_specs.append(spec)
        out_specs.append(spec)
        out_shape.append(jax.ShapeDtypeStruct(w.shape, BF16))
        operands.append(w)
    return pl.pallas_call(
        functools.partial(_ffn_body, tm=tm, n_transpose=len(transpose_jobs),
                          n_cast=len(cast_jobs)),
        grid=(t // tm, nj),
        in_specs=in_specs,
        out_specs=out_specs,
        out_shape=out_shape,
        scratch_shapes=[
            pltpu.VMEM((tm, D_MODEL), BF16),
            pltpu.VMEM((tm, D_MODEL), F32),
            pltpu.SemaphoreType.DMA(()),
        ],
        compiler_params=_params(2, FFN_VMEM_LIMIT_BYTES),
        name=name,
    )(*operands)


def _causal_conv(buf_ref, w_ref, n_taps):
    ext = buf_ref[...]
    acc = None
    for k in range(n_taps):
        delay = n_taps - 1 - k
        shifted = ext if delay == 0 else pltpu.roll(ext, delay, 0)
        tap = shifted[SUBLANES:, :] * w_ref[k:k + 1, :]
        acc = tap if acc is None else acc + tap
    return acc


_BODY = pl.ds(SUBLANES, CHUNK)
_TAIL = pl.ds(CHUNK, SUBLANES)
_HEAD = pl.ds(0, SUBLANES)


def _proj(xn, w_ref):
    return jnp.dot(xn, w_ref[...], preferred_element_type=F32)


def _step_rows():
    return pl.BlockSpec((STEP_ROWS, D_MODEL), lambda s: (s, 0))


def _conv_mix_body(x_ref, res_ref, nw_ref, wb_ref, wc_ref, wx_ref, scw_ref, wout_ref, o_ref,
                   pbuf, *, steps_per_seq):
    @pl.when(pl.program_id(0) % steps_per_seq == 0)
    def _():
        pbuf[_HEAD, :] = jnp.zeros((SUBLANES, CONV_WIDTH), F32)

    for r in range(0, STEP_ROWS, CHUNK):
        rows = pl.ds(r, CHUNK)
        xn = _rmsnorm(x_ref[rows, :], nw_ref[...]).astype(BF16)
        pbuf[_BODY, :] = _proj(xn, wc_ref) * _proj(xn, wx_ref)
        y_a = _proj(xn, wb_ref) * _causal_conv(pbuf, scw_ref, SC_KERNEL)
        pbuf[_HEAD, :] = pbuf[_TAIL, :]
        o_ref[rows, :] = res_ref[rows, :] + jnp.dot(
            y_a.astype(BF16), wout_ref[...], preferred_element_type=F32)


def _conv_mix(h, h_res, norm_w, w_in, sc_w, w_out, *, seq):
    t = h.shape[0]
    return pl.pallas_call(
        functools.partial(_conv_mix_body, steps_per_seq=seq // STEP_ROWS),
        grid=(t // STEP_ROWS,),
        in_specs=[
            _step_rows(), _step_rows(),
            _resident((1, D_MODEL)),
            _resident((D_MODEL, CONV_WIDTH), (0, 0)),
            _resident((D_MODEL, CONV_WIDTH), (0, 1)),
            _resident((D_MODEL, CONV_WIDTH), (0, 2)),
            _resident((SC_KERNEL, CONV_WIDTH)),
            _resident((CONV_WIDTH, D_MODEL), (0, 0)),
        ],
        out_specs=pl.BlockSpec((STEP_ROWS, D_MODEL), lambda s: (s, 0)),
        out_shape=jax.ShapeDtypeStruct((t, D_MODEL), F32),
        scratch_shapes=[pltpu.VMEM((CHUNK + SUBLANES, CONV_WIDTH), F32)],
        compiler_params=_params(1),
        name="conv_mix",
    )(h, h_res, norm_w, w_in, w_in, w_in, sc_w, w_out)


def _split_cat(a, axis, n_terms):
    terms = []
    rest = a
    for _ in range(n_terms):
        term = rest.astype(BF16)
        terms.append(term)
        rest = rest - term.astype(F32)
    return jnp.concatenate(terms, axis=axis)


def _ssd_chunk(xn, wz_ref, wx_ref, wbc_ref, wdt_ref, cwx_ref, cwbc_ref, cbx_ref, cbbc_ref,
               dtb_ref, alog_ref, dsk_ref, gnw_ref, e64_ref, xbuf, bcbuf, st_ref):
    xbuf[_BODY, :] = _proj(xn, wx_ref)
    xc = _silu(_causal_conv(xbuf, cwx_ref, SSD_CONV) + cbx_ref[...])
    xbuf[_HEAD, :] = xbuf[_TAIL, :]
    bcbuf[_BODY, :] = _proj(xn, wbc_ref)
    bcc = _silu(_causal_conv(bcbuf, cwbc_ref, SSD_CONV) + cbbc_ref[...])
    bcbuf[_HEAD, :] = bcbuf[_TAIL, :]

    dt_in = _proj(xn, wdt_ref) + dtb_ref[...]
    dtv = jnp.maximum(dt_in, 0.0) + jnp.log1p(jnp.exp(-jnp.abs(dt_in)))
    a_neg = -jnp.exp(alog_ref[...])
    row = lax.broadcasted_iota(jnp.int32, (CHUNK, CHUNK), 0)
    col = lax.broadcasted_iota(jnp.int32, (CHUNK, CHUNK), 1)
    causal = col <= row
    tril = causal.astype(BF16)
    acs = jnp.dot(jnp.concatenate([tril] * LOG_SPLIT, axis=1),
                  _split_cat(dtv * a_neg, 0, LOG_SPLIT),
                  preferred_element_type=F32)
    acs_t = acs.T
    decay = jnp.exp(acs[CHUNK - 1:CHUNK, :] - acs)
    exp_acs = jnp.exp(acs)

    per_head = jnp.concatenate([dtv, dtv * decay, exp_acs], axis=0)
    per_chan = jnp.dot(_split_cat(per_head, 1, SCALE_SPLIT), e64_ref[...],
                       preferred_element_type=F32)
    dt_e = per_chan[0:CHUNK]
    dt_dec_e = per_chan[CHUNK:2 * CHUNK]
    exp_acs_e = per_chan[2 * CHUNK:3 * CHUNK]
    chunk_decay = exp_acs_e[CHUNK - 1:CHUNK, :]

    x_dt_b = (xc * dt_e).astype(BF16)
    x_dec_b = (xc * dt_dec_e).astype(BF16)

    lane_head = lax.broadcasted_iota(jnp.int32, (CHUNK, QUAD_W), 1) // SSD_HEAD_DIM
    y_groups = []
    for g in range(SSD_GROUPS):
        gsl = slice(g * GROUP_W, (g + 1) * GROUP_W)
        b_g = bcc[:, g * SSD_STATE:(g + 1) * SSD_STATE]
        c_g = bcc[:, SSD_BC + g * SSD_STATE:SSD_BC + (g + 1) * SSD_STATE]
        b_gb = b_g.astype(BF16)
        c_gb = c_g.astype(BF16)
        cb = lax.dot_general(c_gb, b_gb, (((1,), (1,)), ((), ())),
                             preferred_element_type=F32)
        cb = jnp.where(causal, cb, 0.0)
        states = jnp.dot(b_g.T.astype(BF16), x_dec_b[:, gsl], preferred_element_type=F32)
        prev = st_ref[:, gsl]
        y_off = jnp.dot(c_gb, prev.astype(BF16), preferred_element_type=F32) * exp_acs_e[:, gsl]
        st_ref[:, gsl] = prev * chunk_decay[:, gsl] + states
        y_diag = []
        for q in range(HEADS_PER_GROUP // QUAD):
            h0 = g * HEADS_PER_GROUP + q * QUAD
            m_parts = []
            x_parts = []
            x_q = x_dt_b[:, h0 * SSD_HEAD_DIM:h0 * SSD_HEAD_DIM + QUAD_W]
            for hh in range(QUAD):
                h = h0 + hh
                diff = jnp.broadcast_to(acs[:, h:h + 1], (CHUNK, CHUNK)) - acs_t[h:h + 1, :]
                m_parts.append((cb * jnp.exp(jnp.where(causal, diff, 0.0))).astype(BF16))
                x_parts.append(jnp.where(lane_head == hh, x_q, jnp.zeros_like(x_q)))
            m_q = jnp.concatenate(m_parts, axis=1)
            x_bd = jnp.concatenate(x_parts, axis=0)
            y_diag.append(jnp.dot(m_q, x_bd, preferred_element_type=F32))
        y_groups.append(jnp.concatenate(y_diag, axis=1) + y_off)
    y = jnp.concatenate(y_groups, axis=1) + dsk_ref[...] * xc

    gz = y * _silu(_proj(xn, wz_ref))
    gnw = gnw_ref[...]
    out = []
    for g in range(SSD_GROUPS):
        gsl = slice(g * GROUP_W, (g + 1) * GROUP_W)
        gg = gz[:, gsl]
        out.append(gg * lax.rsqrt(jnp.mean(gg * gg, axis=-1, keepdims=True) + EPS) * gnw[:, gsl])
    return jnp.concatenate(out, axis=1)


def _ssd_mix_body(x_ref, nw_ref, wz_ref, wx_ref, wbc_ref, wdt_ref,
                  cwx_ref, cwbc_ref, cbx_ref, cbbc_ref,
                  dtb_ref, alog_ref, dsk_ref, gnw_ref, e64_ref, wout_ref, o_ref,
                  xbuf, bcbuf, st_ref, *, steps_per_seq):
    @pl.when(pl.program_id(0) % steps_per_seq == 0)
    def _():
        xbuf[_HEAD, :] = jnp.zeros((SUBLANES, SSD_WIDTH), F32)
        bcbuf[_HEAD, :] = jnp.zeros((SUBLANES, 2 * SSD_BC), F32)
        st_ref[...] = jnp.zeros_like(st_ref)

    for r in range(0, STEP_ROWS, CHUNK):
        rows = pl.ds(r, CHUNK)
        xn = _rmsnorm(x_ref[rows, :], nw_ref[...]).astype(BF16)
        y_b = _ssd_chunk(xn, wz_ref, wx_ref, wbc_ref, wdt_ref, cwx_ref, cwbc_ref, cbx_ref,
                         cbbc_ref, dtb_ref, alog_ref, dsk_ref, gnw_ref, e64_ref,
                         xbuf, bcbuf, st_ref)
        o_ref[rows, :] = x_ref[rows, :] + jnp.dot(
            y_b.astype(BF16), wout_ref[...], preferred_element_type=F32)


def _ssd_mix(h, norm_w, w_in, w_dt, cw_x, cw_bc, cb_x, cb_bc, dt_bias, a_log, d_skip,
             gnorm_w, e64, w_out, *, seq):
    t = h.shape[0]
    return pl.pallas_call(
        functools.partial(_ssd_mix_body, steps_per_seq=seq // STEP_ROWS),
        grid=(t // STEP_ROWS,),
        in_specs=[
            _step_rows(),
            _resident((1, D_MODEL)),
            _resident((D_MODEL, SSD_WIDTH), (0, CONV_IN_COLS // SSD_WIDTH)),
            _resident((D_MODEL, SSD_WIDTH), (0, CONV_IN_COLS // SSD_WIDTH + 1)),
            _resident((D_MODEL, 2 * SSD_BC), (0, (CONV_IN_COLS + 2 * SSD_WIDTH) // (2 * SSD_BC))),
            _resident((D_MODEL, LANES)),
            _resident((SSD_CONV, SSD_WIDTH)),
            _resident((SSD_CONV, 2 * SSD_BC)),
            _resident((1, SSD_WIDTH)),
            _resident((1, 2 * SSD_BC)),
            _resident((1, LANES)),
            _resident((1, LANES)),
            _resident((1, SSD_WIDTH)),
            _resident((1, SSD_WIDTH)),
            _resident((SCALE_SPLIT * LANES, SSD_WIDTH)),
            _resident((SSD_WIDTH, D_MODEL), (1, 0)),
        ],
        out_specs=pl.BlockSpec((STEP_ROWS, D_MODEL), lambda s: (s, 0)),
        out_shape=jax.ShapeDtypeStruct((t, D_MODEL), F32),
        scratch_shapes=[
            pltpu.VMEM((CHUNK + SUBLANES, SSD_WIDTH), F32),
            pltpu.VMEM((CHUNK + SUBLANES, 2 * SSD_BC), F32),
            pltpu.VMEM((SSD_STATE, SSD_WIDTH), F32),
        ],
        compiler_params=_params(1),
        name="ssd_mix",
    )(h, norm_w, w_in, w_in, w_in, w_dt, cw_x, cw_bc, cb_x, cb_bc, dt_bias, a_log,
      d_skip, gnorm_w, e64, w_out)


def _transpose_cast_body(w_ref, o_ref):
    o_ref[...] = w_ref[...].T.astype(BF16)


def _transpose_cast(w_t, *, n_rows, bn):
    d = w_t.shape[1]
    return pl.pallas_call(
        _transpose_cast_body,
        grid=(n_rows // bn,),
        in_specs=[pl.BlockSpec((bn, d), lambda j: (j, 0))],
        out_specs=pl.BlockSpec((d, bn), lambda j: (0, j)),
        out_shape=jax.ShapeDtypeStruct((d, n_rows), BF16),
        compiler_params=_params(1),
        name="transpose_cast",
    )(w_t)


def _ple_final_body(h_ref, p_ref, pnw_ref, wg_ref, wp_ref, fnw_ref, o_ref, *, last_layer):
    h = h_ref[...]
    xn = _rmsnorm(h, pnw_ref[...]).astype(BF16)
    gate = jax.nn.sigmoid(jnp.dot(xn, wg_ref[...], preferred_element_type=F32))
    proj = jnp.dot(p_ref[...].astype(BF16), wp_ref[...], preferred_element_type=F32)
    h = h + gate * proj
    o_ref[...] = _rmsnorm(h, fnw_ref[...]) if last_layer else h


def _ple_final(h, p, ple_norm, w_gate, w_proj, final_norm, *, tm, last_layer):
    t = h.shape[0]
    ple_dim = p.shape[1]
    return pl.pallas_call(
        functools.partial(_ple_final_body, last_layer=last_layer),
        grid=(t // tm,),
        in_specs=[
            pl.BlockSpec((tm, D_MODEL), lambda i: (i, 0)),
            pl.BlockSpec((tm, ple_dim), lambda i: (i, 0)),
            pl.BlockSpec((1, D_MODEL), lambda i: (0, 0)),
            pl.BlockSpec((D_MODEL, D_MODEL), lambda i: (0, 0)),
            pl.BlockSpec((ple_dim, D_MODEL), lambda i: (0, 0)),
            pl.BlockSpec((1, D_MODEL), lambda i: (0, 0)),
        ],
        out_specs=pl.BlockSpec((tm, D_MODEL), lambda i: (i, 0)),
        out_shape=jax.ShapeDtypeStruct((t, D_MODEL), F32),
        compiler_params=_params(1),
        name="ple_final",
    )(h, p, ple_norm, w_gate, w_proj, final_norm)


def _head_selector(width, n_terms):
    src = jnp.arange(LANES, dtype=jnp.int32)[:, None]
    dst = jnp.arange(SSD_HEADS * width, dtype=jnp.int32)[None, :] // width
    return jnp.tile((src == dst).astype(BF16), (n_terms, 1))


def _pad_lanes(v):
    return jnp.pad(v.astype(F32), (0, LANES - v.shape[0]))[None, :]


def kernel(x, p, ffn1_norm, ffn1_w_in, ffn1_w_out, mix_norm, mix_w_in, sc_conv_w, ssd_conv_w,
           ssd_conv_b, ssd_dt_bias, ssd_a_log, ssd_d, ssd_norm, mix_w_out, ffn2_norm, ffn2_w_in,
           ffn2_w_out, ple_norm, ple_w_gate, ple_w_proj, final_norm):
    batch, seq, _ = x.shape
    depth = ffn1_norm.shape[0]
    t = batch * seq
    h = x.reshape(t, D_MODEL)
    e64 = _head_selector(SSD_HEAD_DIM, SCALE_SPLIT)
    row = lambda v: v.astype(F32)[None, :]

    for i in range(depth):
        w_in_t = jnp.swapaxes(mix_w_in, 1, 2)[i]
        h, w_in, w_out = _ffn(h, row(ffn1_norm[i]), ffn1_w_in[i], ffn1_w_out[i],
                              tm=1024, tf=512, name="ffn1", transpose_jobs=[w_in_t],
                              cast_jobs=[(mix_w_out[i], 64)])
        assert w_in.shape[1] == MAIN_COLS
        w_dt = _transpose_cast(jnp.pad(w_in_t[MAIN_COLS:], ((0, LANES - SSD_HEADS), (0, 0))),
                               n_rows=LANES, bn=LANES)
        cw = ssd_conv_w[i].astype(F32)
        cb = ssd_conv_b[i].astype(F32)[None, :]
        h_ssd = _ssd_mix(h, row(mix_norm[i]), w_in, w_dt,
                         cw[:, :SSD_WIDTH], cw[:, SSD_WIDTH:], cb[:, :SSD_WIDTH], cb[:, SSD_WIDTH:],
                         _pad_lanes(ssd_dt_bias[i]), _pad_lanes(ssd_a_log[i]),
                         jnp.repeat(ssd_d[i].astype(F32), SSD_HEAD_DIM)[None, :],
                         row(ssd_norm[i]), e64, w_out, seq=seq)
        h = _conv_mix(h, h_ssd, row(mix_norm[i]), w_in, sc_conv_w[i].astype(F32), w_out, seq=seq)

        h, ple_gate_w = _ffn(h, row(ffn2_norm[i]), ffn2_w_in[i], ffn2_w_out[i],
                             tm=1024, tf=512, name="ffn2", cast_jobs=[(ple_w_gate[i], 32)])

        h = _ple_final(h, p[i].reshape(t, -1), row(ple_norm[i]), ple_gate_w,
                       ple_w_proj[i].astype(BF16), row(final_norm), tm=512,
                       last_layer=(i + 1 == depth))
    return h.reshape(batch, seq, D_MODEL)
```

```python
import functools

import jax
import jax.numpy as jnp
from jax import lax
from jax.experimental import pallas as pl
from jax.experimental.pallas import tpu as pltpu

F32 = jnp.float32
BF16 = jnp.bfloat16

D_MODEL = 2048
D_FF = 5632
CONV_WIDTH = 2048
SC_KERNEL = 3
SSD_HEADS = 32
SSD_HEAD_DIM = 64
SSD_WIDTH = SSD_HEADS * SSD_HEAD_DIM
SSD_STATE = 128
SSD_GROUPS = 4
SSD_CONV = 4
CHUNK = 128
SSD_BC = SSD_GROUPS * SSD_STATE
SSD_XBC = SSD_WIDTH + 2 * SSD_BC
EPS = 1e-6

LANES = 128
SUBLANES = 8
VMEM_LIMIT_BYTES = 60000 * 1024
FFN_VMEM_LIMIT_BYTES = 127 * 512 * 1024
GROUP_W = SSD_WIDTH // SSD_GROUPS
HEADS_PER_GROUP = SSD_HEADS // SSD_GROUPS
QUAD = 4
QUAD_W = QUAD * SSD_HEAD_DIM
LOG_SPLIT = 3
SCALE_SPLIT = 2

CONV_IN_COLS = 3 * CONV_WIDTH
MAIN_COLS = CONV_IN_COLS + SSD_WIDTH + SSD_XBC
STEP_ROWS = 2 * CHUNK


def _params(n_axes, vmem_limit_bytes=VMEM_LIMIT_BYTES):
    return pltpu.CompilerParams(
        dimension_semantics=("arbitrary",) * n_axes,
        vmem_limit_bytes=vmem_limit_bytes)


def _rmsnorm(x, w):
    return x * lax.rsqrt(jnp.mean(x * x, axis=-1, keepdims=True) + EPS) * w


def _silu(x):
    return x * jax.nn.sigmoid(x)


def _resident(shape, block_index=None):
    index = (0,) * len(shape) if block_index is None else block_index
    return pl.BlockSpec(shape, lambda *_: index, pipeline_mode=pl.Buffered(1))


def _ffn_body(x_hbm, nw_ref, wg_ref, wu_ref, wo_ref, *rest, tm, n_transpose, n_cast):
    n_side = n_transpose + n_cast
    side_in = rest[:n_side]
    o_ref = rest[n_side]
    side_out = rest[n_side + 1:2 * n_side + 1]
    xn_ref, xbuf, sem = rest[2 * n_side + 1:]
    i = pl.program_id(0)
    j = pl.program_id(1)

    def x_copy(tile):
        return pltpu.make_async_copy(x_hbm.at[pl.ds(tile * tm, tm), :], xbuf, sem)

    @pl.when((i == 0) & (j == 0))
    def _():
        x_copy(0).start()

    @pl.when(j == 0)
    def _():
        x_copy(i).wait()
        nw = nw_ref[...]

        def slab(r, carry):
            rows = pl.ds(pl.multiple_of(r * CHUNK, CHUNK), CHUNK)
            x = xbuf[rows, :]
            xn_ref[rows, :] = _rmsnorm(x, nw).astype(BF16)
            o_ref[rows, :] = x
            return carry

        lax.fori_loop(0, tm // CHUNK, slab, None)

    @pl.when((j == 1) & (i + 1 < pl.num_programs(0)))
    def _():
        x_copy(i + 1).start()

    for k in range(n_side):
        block = side_in[k][...]
        side_out[k][...] = (block.T if k < n_transpose else block).astype(BF16)
    xn = xn_ref[...]
    g = jnp.dot(xn, wg_ref[...].astype(BF16), preferred_element_type=F32)
    u = jnp.dot(xn, wu_ref[...].astype(BF16), preferred_element_type=F32)
    a = (0.5 * _silu(g) * u).astype(BF16)
    o_ref[...] += jnp.dot(a, wo_ref[...].astype(BF16), preferred_element_type=F32)


SIDE_ROWS = 128


def _ffn(h, norm_w, w_in, w_out, *, tm, tf, name, transpose_jobs=(), cast_jobs=()):
    t = h.shape[0]
    nj = D_FF // tf
    assert nj >= 2, "the x prefetch is issued in the second d_ff step"
    in_specs = [
        pl.BlockSpec(memory_space=pl.ANY),
        pl.BlockSpec((1, D_MODEL), lambda i, j: (0, 0)),
        pl.BlockSpec((D_MODEL, tf), lambda i, j: (0, j)),
        pl.BlockSpec((D_MODEL, tf), lambda i, j: (0, j + nj)),
        pl.BlockSpec((tf, D_MODEL), lambda i, j: (j, 0)),
    ]
    out_specs = [pl.BlockSpec((tm, D_MODEL), lambda i, j: (i, 0))]
    out_shape = [jax.ShapeDtypeStruct((t, D_MODEL), F32)]
    operands = [h, norm_w, w_in, w_in, w_out]
    n_steps = (t // tm) * nj
    for w_t in transpose_jobs:
        assert n_steps * SIDE_ROWS <= w_t.shape[0]
        in_specs.append(pl.BlockSpec((SIDE_ROWS, D_MODEL), lambda i, j: (i * nj + j, 0)))
        out_specs.append(pl.BlockSpec((D_MODEL, SIDE_ROWS), lambda i, j: (0, i * nj + j)))
        out_shape.append(jax.ShapeDtypeStruct((D_MODEL, n_steps * SIDE_ROWS), BF16))
        operands.append(w_t)
    for w, rows in cast_jobs:
        last = w.shape[0] // rows - 1
        assert w.shape[0] % rows == 0 and last < n_steps
        spec = pl.BlockSpec((rows, w.shape[1]),
                            lambda i, j, last=last: (jnp.minimum(i * nj + j, last), 0))
        in_specs.append(spec)
        out_specs.append(spec)
        out_shape.append(jax.ShapeDtypeStruct(w.shape, BF16))
        operands.append(w)
    return pl.pallas_call(
        functools.partial(_ffn_body, tm=tm, n_transpose=len(transpose_jobs),
                          n_cast=len(cast_jobs)),
        grid=(t // tm, nj),
        in_specs=in_specs,
        out_specs=out_specs,
        out_shape=out_shape,
        scratch_shapes=[
            pltpu.VMEM((tm, D_MODEL), BF16),
            pltpu.VMEM((tm, D_MODEL), F32),
            pltpu.SemaphoreType.DMA(()),
        ],
        compiler_params=_params(2, FFN_VMEM_LIMIT_BYTES),
        name=name,
    )(*operands)


def _causal_conv(buf_ref, w_ref, n_taps):
    ext = buf_ref[...]
    acc = None
    for k in range(n_taps):
        delay = n_taps - 1 - k
        shifted = ext if delay == 0 else pltpu.roll(ext, delay, 0)
        tap = shifted[SUBLANES:, :] * w_ref[k:k + 1, :]
        acc = tap if acc is None else acc + tap
    return acc


_BODY = pl.ds(SUBLANES, CHUNK)
_TAIL = pl.ds(CHUNK, SUBLANES)
_HEAD = pl.ds(0, SUBLANES)


def _proj(xn, w_ref):
    return jnp.dot(xn, w_ref[...], preferred_element_type=F32)


def _step_rows():
    return pl.BlockSpec((STEP_ROWS, D_MODEL), lambda s: (s, 0))


def _conv_mix_body(x_ref, res_ref, nw_ref, wb_ref, wc_ref, wx_ref, scw_ref, wout_ref, o_ref,
                   pbuf, *, steps_per_seq):
    @pl.when(pl.program_id(0) % steps_per_seq == 0)
    def _():
        pbuf[_HEAD, :] = jnp.zeros((SUBLANES, CONV_WIDTH), F32)

    for r in range(0, STEP_ROWS, CHUNK):
        rows = pl.ds(r, CHUNK)
        xn = _rmsnorm(x_ref[rows, :], nw_ref[...]).astype(BF16)
        pbuf[_BODY, :] = _proj(xn, wc_ref) * _proj(xn, wx_ref)
        y_a = _proj(xn, wb_ref) * _causal_conv(pbuf, scw_ref, SC_KERNEL)
        pbuf[_HEAD, :] = pbuf[_TAIL, :]
        o_ref[rows, :] = res_ref[rows, :] + jnp.dot(
            y_a.astype(BF16), wout_ref[...], preferred_element_type=F32)


def _conv_mix(h, h_res, norm_w, w_in, sc_w, w_out, *, seq):
    t = h.shape[0]
    return pl.pallas_call(
        functools.partial(_conv_mix_body, steps_per_seq=seq // STEP_ROWS),
        grid=(t // STEP_ROWS,),
        in_specs=[
            _step_rows(), _step_rows(),
            _resident((1, D_MODEL)),
            _resident((D_MODEL, CONV_WIDTH), (0, 0)),
            _resident((D_MODEL, CONV_WIDTH), (0, 1)),
            _resident((D_MODEL, CONV_WIDTH), (0, 2)),
            _resident((SC_KERNEL, CONV_WIDTH)),
            _resident((CONV_WIDTH, D_MODEL), (0, 0)),
        ],
        out_specs=pl.BlockSpec((STEP_ROWS, D_MODEL), lambda s: (s, 0)),
        out_shape=jax.ShapeDtypeStruct((t, D_MODEL), F32),
        scratch_shapes=[pltpu.VMEM((CHUNK + SUBLANES, CONV_WIDTH), F32)],
        compiler_params=_params(1),
        name="conv_mix",
    )(h, h_res, norm_w, w_in, w_in, w_in, sc_w, w_out)


def _split_cat(a, axis, n_terms):
    terms = []
    rest = a
    for _ in range(n_terms):
        term = rest.astype(BF16)
        terms.append(term)
        rest = rest - term.astype(F32)
    return jnp.concatenate(terms, axis=axis)


def _ssd_chunk(xn, wz_ref, wx_ref, wbc_ref, wdt_ref, cwx_ref, cwbc_ref, cbx_ref, cbbc_ref,
               dtb_ref, alog_ref, dsk_ref, gnw_ref, e64_ref, xbuf, bcbuf, st_ref):
    xbuf[_BODY, :] = _proj(xn, wx_ref)
    xc = _silu(_causal_conv(xbuf, cwx_ref, SSD_CONV) + cbx_ref[...])
    xbuf[_HEAD, :] = xbuf[_TAIL, :]
    bcbuf[_BODY, :] = _proj(xn, wbc_ref)
    bcc = _silu(_causal_conv(bcbuf, cwbc_ref, SSD_CONV) + cbbc_ref[...])
    bcbuf[_HEAD, :] = bcbuf[_TAIL, :]

    dt_in = _proj(xn, wdt_ref) + dtb_ref[...]
    dtv = jnp.maximum(dt_in, 0.0) + jnp.log1p(jnp.exp(-jnp.abs(dt_in)))
    a_neg = -jnp.exp(alog_ref[...])
    row = lax.broadcasted_iota(jnp.int32, (CHUNK, CHUNK), 0)
    col = lax.broadcasted_iota(jnp.int32, (CHUNK, CHUNK), 1)
    causal = col <= row
    tril = causal.astype(BF16)
    acs = jnp.dot(jnp.concatenate([tril] * LOG_SPLIT, axis=1),
                  _split_cat(dtv * a_neg, 0, LOG_SPLIT),
                  preferred_element_type=F32)
    acs_t = acs.T
    decay = jnp.exp(acs[CHUNK - 1:CHUNK, :] - acs)
    exp_acs = jnp.exp(acs)

    per_head = jnp.concatenate([dtv, dtv * decay, exp_acs], axis=0)
    per_chan = jnp.dot(_split_cat(per_head, 1, SCALE_SPLIT), e64_ref[...],
                       preferred_element_type=F32)
    dt_e = per_chan[0:CHUNK]
    dt_dec_e = per_chan[CHUNK:2 * CHUNK]
    exp_acs_e = per_chan[2 * CHUNK:3 * CHUNK]
    chunk_decay = exp_acs_e[CHUNK - 1:CHUNK, :]

    x_dt_b = (xc * dt_e).astype(BF16)
    x_dec_b = (xc * dt_dec_e).astype(BF16)

    lane_head = lax.broadcasted_iota(jnp.int32, (CHUNK, QUAD_W), 1) // SSD_HEAD_DIM
    y_groups = []
    for g in range(SSD_GROUPS):
        gsl = slice(g * GROUP_W, (g + 1) * GROUP_W)
        b_g = bcc[:, g * SSD_STATE:(g + 1) * SSD_STATE]
        c_g = bcc[:, SSD_BC + g * SSD_STATE:SSD_BC + (g + 1) * SSD_STATE]
        b_gb = b_g.astype(BF16)
        c_gb = c_g.astype(BF16)
        cb = lax.dot_general(c_gb, b_gb, (((1,), (1,)), ((), ())),
                             preferred_element_type=F32)
        cb = jnp.where(causal, cb, 0.0)
        states = jnp.dot(b_g.T.astype(BF16), x_dec_b[:, gsl], preferred_element_type=F32)
        prev = st_ref[:, gsl]
        y_off = jnp.dot(c_gb, prev.astype(BF16), preferred_element_type=F32) * exp_acs_e[:, gsl]
        st_ref[:, gsl] = prev * chunk_decay[:, gsl] + states
        y_diag = []
        for q in range(HEADS_PER_GROUP // QUAD):
            h0 = g * HEADS_PER_GROUP + q * QUAD
            m_parts = []
            x_parts = []
            x_q = x_dt_b[:, h0 * SSD_HEAD_DIM:h0 * SSD_HEAD_DIM + QUAD_W]
            for hh in range(QUAD):
                h = h0 + hh
                diff = jnp.broadcast_to(acs[:, h:h + 1], (CHUNK, CHUNK)) - acs_t[h:h + 1, :]
                m_parts.append((cb * jnp.exp(jnp.where(causal, diff, 0.0))).astype(BF16))
                x_parts.append(jnp.where(lane_head == hh, x_q, jnp.zeros_like(x_q)))
            m_q = jnp.concatenate(m_parts, axis=1)
            x_bd = jnp.concatenate(x_parts, axis=0)
            y_diag.append(jnp.dot(m_q, x_bd, preferred_element_type=F32))
        y_groups.append(jnp.concatenate(y_diag, axis=1) + y_off)
    y = jnp.concatenate(y_groups, axis=1) + dsk_ref[...] * xc

    gz = y * _silu(_proj(xn, wz_ref))
    gnw = gnw_ref[...]
    out = []
    for g in range(SSD_GROUPS):
        gsl = slice(g * GROUP_W, (g + 1) * GROUP_W)
        gg = gz[:, gsl]
        out.append(gg * lax.rsqrt(jnp.mean(gg * gg, axis=-1, keepdims=True) + EPS) * gnw[:, gsl])
    return jnp.concatenate(out, axis=1)


def _ssd_mix_body(x_ref, nw_ref, wz_ref, wx_ref, wbc_ref, wdt_ref,
                  cwx_ref, cwbc_ref, cbx_ref, cbbc_ref,
                  dtb_ref, alog_ref, dsk_ref, gnw_ref, e64_ref, wout_ref, o_ref,
                  xbuf, bcbuf, st_ref, *, steps_per_seq):
    @pl.when(pl.program_id(0) % steps_per_seq == 0)
    def _():
        xbuf[_HEAD, :] = jnp.zeros((SUBLANES, SSD_WIDTH), F32)
        bcbuf[_HEAD, :] = jnp.zeros((SUBLANES, 2 * SSD_BC), F32)
        st_ref[...] = jnp.zeros_like(st_ref)

    for r in range(0, STEP_ROWS, CHUNK):
        rows = pl.ds(r, CHUNK)
        xn = _rmsnorm(x_ref[rows, :], nw_ref[...]).astype(BF16)
        y_b = _ssd_chunk(xn, wz_ref, wx_ref, wbc_ref, wdt_ref, cwx_ref, cwbc_ref, cbx_ref,
                         cbbc_ref, dtb_ref, alog_ref, dsk_ref, gnw_ref, e64_ref,
                         xbuf, bcbuf, st_ref)
        o_ref[rows, :] = x_ref[rows, :] + jnp.dot(
            y_b.astype(BF16), wout_ref[...], preferred_element_type=F32)


def _ssd_mix(h, norm_w, w_in, w_dt, cw_x, cw_bc, cb_x, cb_bc, dt_bias, a_log, d_skip,
             gnorm_w, e64, w_out, *, seq):
    t = h.shape[0]
    return pl.pallas_call(
        functools.partial(_ssd_mix_body, steps_per_seq=seq // STEP_ROWS),
        grid=(t // STEP_ROWS,),
        in_specs=[
            _step_rows(),
            _resident((1, D_MODEL)),
            _resident((D_MODEL, SSD_WIDTH), (0, CONV_IN_COLS // SSD_WIDTH)),
            _resident((D_MODEL, SSD_WIDTH), (0, CONV_IN_COLS // SSD_WIDTH + 1)),
            _resident((D_MODEL, 2 * SSD_BC), (0, (CONV_IN_COLS + 2 * SSD_WIDTH) // (2 * SSD_BC))),
            _resident((D_MODEL, LANES)),
            _resident((SSD_CONV, SSD_WIDTH)),
            _resident((SSD_CONV, 2 * SSD_BC)),
            _resident((1, SSD_WIDTH)),
            _resident((1, 2 * SSD_BC)),
            _resident((1, LANES)),
            _resident((1, LANES)),
            _resident((1, SSD_WIDTH)),
            _resident((1, SSD_WIDTH)),
            _resident((SCALE_SPLIT * LANES, SSD_WIDTH)),
            _resident((SSD_WIDTH, D_MODEL), (1, 0)),
        ],
        out_specs=pl.BlockSpec((STEP_ROWS, D_MODEL), lambda s: (s, 0)),
        out_shape=jax.ShapeDtypeStruct((t, D_MODEL), F32),
        scratch_shapes=[
            pltpu.VMEM((CHUNK + SUBLANES, SSD_WIDTH), F32),
            pltpu.VMEM((CHUNK + SUBLANES, 2 * SSD_BC), F32),
            pltpu.VMEM((SSD_STATE, SSD_WIDTH), F32),
        ],
        compiler_params=_params(1),
        name="ssd_mix",
    )(h, norm_w, w_in, w_in, w_in, w_dt, cw_x, cw_bc, cb_x, cb_bc, dt_bias, a_log,
      d_skip, gnorm_w, e64, w_out)


def _transpose_cast_body(w_ref, o_ref):
    o_ref[...] = w_ref[...].T.astype(BF16)


def _transpose_cast(w_t, *, n_rows, bn):
    d = w_t.shape[1]
    return pl.pallas_call(
        _transpose_cast_body,
        grid=(n_rows // bn,),
        in_specs=[pl.BlockSpec((bn, d), lambda j: (j, 0))],
        out_specs=pl.BlockSpec((d, bn), lambda j: (0, j)),
        out_shape=jax.ShapeDtypeStruct((d, n_rows), BF16),
        compiler_params=_params(1),
        name="transpose_cast",
    )(w_t)


def _ple_final_body(h_ref, p_ref, pnw_ref, wg_ref, wp_ref, fnw_ref, o_ref, *, last_layer):
    h = h_ref[...]
    xn = _rmsnorm(h, pnw_ref[...]).astype(BF16)
    gate = jax.nn.sigmoid(jnp.dot(xn, wg_ref[...], preferred_element_type=F32))
    proj = jnp.dot(p_ref[...].astype(BF16), wp_ref[...], preferred_element_type=F32)
    h = h + gate * proj
    o_ref[...] = _rmsnorm(h, fnw_ref[...]) if last_layer else h


def _ple_final(h, p, ple_norm, w_gate, w_proj, final_norm, *, tm, last_layer):
    t = h.shape[0]
    ple_dim = p.shape[1]
    return pl.pallas_call(
        functools.partial(_ple_final_body, last_layer=last_layer),
        grid=(t // tm,),
        in_specs=[
            pl.BlockSpec((tm, D_MODEL), lambda i: (i, 0)),
            pl.BlockSpec((tm, ple_dim), lambda i: (i, 0)),
            pl.BlockSpec((1, D_MODEL), lambda i: (0, 0)),
            pl.BlockSpec((D_MODEL, D_MODEL), lambda i: (0, 0)),
            pl.BlockSpec((ple_dim, D_MODEL), lambda i: (0, 0)),
            pl.BlockSpec((1, D_MODEL), lambda i: (0, 0)),
        ],
        out_specs=pl.BlockSpec((tm, D_MODEL), lambda i: (i, 0)),
        out_shape=jax.ShapeDtypeStruct((t, D_MODEL), F32),
        compiler_params=_params(1),
        name="ple_final",
    )(h, p, ple_norm, w_gate, w_proj, final_norm)


def _head_selector(width, n_terms):
    src = jnp.arange(LANES, dtype=jnp.int32)[:, None]
    dst = jnp.arange(SSD_HEADS * width, dtype=jnp.int32)[None, :] // width
    return jnp.tile((src == dst).astype(BF16), (n_terms, 1))


def _pad_lanes(v):
    return jnp.pad(v.astype(F32), (0, LANES - v.shape[0]))[None, :]


def kernel(x, p, ffn1_norm, ffn1_w_in, ffn1_w_out, mix_norm, mix_w_in, sc_conv_w, ssd_conv_w,
           ssd_conv_b, ssd_dt_bias, ssd_a_log, ssd_d, ssd_norm, mix_w_out, ffn2_norm, ffn2_w_in,
           ffn2_w_out, ple_norm, ple_w_gate, ple_w_proj, final_norm):
    batch, seq, _ = x.shape
    depth = ffn1_norm.shape[0]
    t = batch * seq
    h = x.reshape(t, D_MODEL)
    e64 = _head_selector(SSD_HEAD_DIM, SCALE_SPLIT)
    row = lambda v: v.astype(F32)[None, :]

    for i in range(depth):
        w_in_t = jnp.swapaxes(mix_w_in, 1, 2)[i]
        h, w_in, w_out = _ffn(h, row(ffn1_norm[i]), ffn1_w_in[i], ffn1_w_out[i],
                              tm=1024, tf=512, name="ffn1", transpose_jobs=[w_in_t],
                              cast_jobs=[(mix_w_out[i], 64)])
        assert w_in.shape[1] == MAIN_COLS
        w_dt = _transpose_cast(jnp.pad(w_in_t[MAIN_COLS:], ((0, LANES - SSD_HEADS), (0, 0))),
                               n_rows=LANES, bn=LANES)
        cw = ssd_conv_w[i].astype(F32)
        cb = ssd_conv_b[i].astype(F32)[None, :]
        h_ssd = _ssd_mix(h, row(mix_norm[i]), w_in, w_dt,
                         cw[:, :SSD_WIDTH], cw[:, SSD_WIDTH:], cb[:, :SSD_WIDTH], cb[:, SSD_WIDTH:],
                         _pad_lanes(ssd_dt_bias[i]), _pad_lanes(ssd_a_log[i]),
                         jnp.repeat(ssd_d[i].astype(F32), SSD_HEAD_DIM)[None, :],
                         row(ssd_norm[i]), e64, w_out, seq=seq)
        h = _conv_mix(h, h_ssd, row(mix_norm[i]), w_in, sc_conv_w[i].astype(F32), w_out, seq=seq)

        h, ple_gate_w = _ffn(h, row(ffn2_norm[i]), ffn2_w_in[i], ffn2_w_out[i],
                             tm=1024, tf=512, name="ffn2", cast_jobs=[(ple_w_gate[i], 32)])

        h = _ple_final(h, p[i].reshape(t, -1), row(ple_norm[i]), ple_gate_w,
                       ple_w_proj[i].astype(BF16), row(final_norm), tm=512,
                       last_layer=(i + 1 == depth))
    return h.reshape(batch, seq, D_MODEL)
```

```python
import functools

import jax
import jax.numpy as jnp
from jax import lax
from jax.experimental import pallas as pl
from jax.experimental.pallas import tpu as pltpu

F32 = jnp.float32
BF16 = jnp.bfloat16

D_MODEL = 2048
D_FF = 5632
CONV_WIDTH = 2048
SC_KERNEL = 3
SSD_HEADS = 32
SSD_HEAD_DIM = 64
SSD_WIDTH = SSD_HEADS * SSD_HEAD_DIM
SSD_STATE = 128
SSD_GROUPS = 4
SSD_CONV = 4
CHUNK = 128
SSD_BC = SSD_GROUPS * SSD_STATE
SSD_XBC = SSD_WIDTH + 2 * SSD_BC
EPS = 1e-6

LANES = 128
SUBLANES = 8
VMEM_LIMIT_BYTES = 60000 * 1024
FFN_VMEM_LIMIT_BYTES = 127 * 512 * 1024
GROUP_W = SSD_WIDTH // SSD_GROUPS
HEADS_PER_GROUP = SSD_HEADS // SSD_GROUPS
QUAD = 4
QUAD_W = QUAD * SSD_HEAD_DIM
LOG_SPLIT = 3
SCALE_SPLIT = 2

CONV_IN_COLS = 3 * CONV_WIDTH
MAIN_COLS = CONV_IN_COLS + SSD_WIDTH + SSD_XBC
STEP_ROWS = 2 * CHUNK


def _params(n_axes, vmem_limit_bytes=VMEM_LIMIT_BYTES):
    return pltpu.CompilerParams(
        dimension_semantics=("arbitrary",) * n_axes,
        vmem_limit_bytes=vmem_limit_bytes)


def _rmsnorm(x, w):
    return x * lax.rsqrt(jnp.mean(x * x, axis=-1, keepdims=True) + EPS) * w


def _silu(x):
    return x * jax.nn.sigmoid(x)


def _resident(shape, block_index=None):
    index = (0,) * len(shape) if block_index is None else block_index
    return pl.BlockSpec(shape, lambda *_: index, pipeline_mode=pl.Buffered(1))


def _ffn_body(x_hbm, nw_ref, wg_ref, wu_ref, wo_ref, *rest, tm, n_transpose, n_cast):
    n_side = n_transpose + n_cast
    side_in = rest[:n_side]
    o_ref = rest[n_side]
    side_out = rest[n_side + 1:2 * n_side + 1]
    xn_ref, xbuf, sem = rest[2 * n_side + 1:]
    i = pl.program_id(0)
    j = pl.program_id(1)

    def x_copy(tile):
        return pltpu.make_async_copy(x_hbm.at[pl.ds(tile * tm, tm), :], xbuf, sem)

    @pl.when((i == 0) & (j == 0))
    def _():
        x_copy(0).start()

    @pl.when(j == 0)
    def _():
        x_copy(i).wait()
        nw = nw_ref[...]

        def slab(r, carry):
            rows = pl.ds(pl.multiple_of(r * NORM_SLAB, NORM_SLAB), NORM_SLAB)
            x = xbuf[rows, :]
            xn_ref[rows, :] = _rmsnorm(x, nw).astype(BF16)
            o_ref[rows, :] = x
            return carry

        lax.fori_loop(0, tm // NORM_SLAB, slab, None)

    @pl.when((j == 1) & (i + 1 < pl.num_programs(0)))
    def _():
        x_copy(i + 1).start()

    for k in range(n_side):
        block = side_in[k][...]
        side_out[k][...] = (block.T if k < n_transpose else block).astype(BF16)
    xn = xn_ref[...]
    g = jnp.dot(xn, wg_ref[...].astype(BF16), preferred_element_type=F32)
    u = jnp.dot(xn, wu_ref[...].astype(BF16), preferred_element_type=F32)
    a = (0.5 * _silu(g) * u).astype(BF16)
    o_ref[...] += jnp.dot(a, wo_ref[...].astype(BF16), preferred_element_type=F32)


SIDE_ROWS = 128
NORM_SLAB = 256


def _ffn(h, norm_w, w_in, w_out, *, tm, tf, name, transpose_jobs=(), cast_jobs=()):
    t = h.shape[0]
    nj = D_FF // tf
    assert nj >= 2, "the x prefetch is issued in the second d_ff step"
    in_specs = [
        pl.BlockSpec(memory_space=pl.ANY),
        pl.BlockSpec((1, D_MODEL), lambda i, j: (0, 0)),
        pl.BlockSpec((D_MODEL, tf), lambda i, j: (0, j)),
        pl.BlockSpec((D_MODEL, tf), lambda i, j: (0, j + nj)),
        pl.BlockSpec((tf, D_MODEL), lambda i, j: (j, 0)),
    ]
    out_specs = [pl.BlockSpec((tm, D_MODEL), lambda i, j: (i, 0))]
    out_shape = [jax.ShapeDtypeStruct((t, D_MODEL), F32)]
    operands = [h, norm_w, w_in, w_in, w_out]
    n_steps = (t // tm) * nj
    for w_t in transpose_jobs:
        assert n_steps * SIDE_ROWS <= w_t.shape[0]
        in_specs.append(pl.BlockSpec((SIDE_ROWS, D_MODEL), lambda i, j: (i * nj + j, 0)))
        out_specs.append(pl.BlockSpec((D_MODEL, SIDE_ROWS), lambda i, j: (0, i * nj + j)))
        out_shape.append(jax.ShapeDtypeStruct((D_MODEL, n_steps * SIDE_ROWS), BF16))
        operands.append(w_t)
    for w, rows in cast_jobs:
        last = w.shape[0] // rows - 1
        assert w.shape[0] % rows == 0 and last < n_steps
        spec = pl.BlockSpec((rows, w.shape[1]),
                            lambda i, j, last=last: (jnp.minimum(i * nj + j, last), 0))
        in_specs.append(spec)
        out_specs.append(spec)
        out_shape.append(jax.ShapeDtypeStruct(w.shape, BF16))
        operands.append(w)
    return pl.pallas_call(
        functools.partial(_ffn_body, tm=tm, n_transpose=len(transpose_jobs),
                          n_cast=len(cast_jobs)),
        grid=(t // tm, nj),
        in_specs=in_specs,
        out_specs=out_specs,
        out_shape=out_shape,
        scratch_shapes=[
            pltpu.VMEM((tm, D_MODEL), BF16),
            pltpu.VMEM((tm, D_MODEL), F32),
            pltpu.SemaphoreType.DMA(()),
        ],
        compiler_params=_params(2, FFN_VMEM_LIMIT_BYTES),
        name=name,
    )(*operands)


def _causal_conv(buf_ref, w_ref, n_taps):
    ext = buf_ref[...]
    acc = None
    for k in range(n_taps):
        delay = n_taps - 1 - k
        shifted = ext if delay == 0 else pltpu.roll(ext, delay, 0)
        tap = shifted[SUBLANES:, :] * w_ref[k:k + 1, :]
        acc = tap if acc is None else acc + tap
    return acc


_BODY = pl.ds(SUBLANES, CHUNK)
_TAIL = pl.ds(CHUNK, SUBLANES)
_HEAD = pl.ds(0, SUBLANES)


def _proj(xn, w_ref):
    return jnp.dot(xn, w_ref[...], preferred_element_type=F32)


def _step_rows():
    return pl.BlockSpec((STEP_ROWS, D_MODEL), lambda s: (s, 0))


def _conv_mix_body(x_ref, res_ref, nw_ref, wb_ref, wc_ref, wx_ref, scw_ref, wout_ref, o_ref,
                   pbuf, *, steps_per_seq):
    @pl.when(pl.program_id(0) % steps_per_seq == 0)
    def _():
        pbuf[_HEAD, :] = jnp.zeros((SUBLANES, CONV_WIDTH), F32)

    for r in range(0, STEP_ROWS, CHUNK):
        rows = pl.ds(r, CHUNK)
        xn = _rmsnorm(x_ref[rows, :], nw_ref[...]).astype(BF16)
        pbuf[_BODY, :] = _proj(xn, wc_ref) * _proj(xn, wx_ref)
        y_a = _proj(xn, wb_ref) * _causal_conv(pbuf, scw_ref, SC_KERNEL)
        pbuf[_HEAD, :] = pbuf[_TAIL, :]
        o_ref[rows, :] = res_ref[rows, :] + jnp.dot(
            y_a.astype(BF16), wout_ref[...], preferred_element_type=F32)


def _conv_mix(h, h_res, norm_w, w_in, sc_w, w_out, *, seq):
    t = h.shape[0]
    return pl.pallas_call(
        functools.partial(_conv_mix_body, steps_per_seq=seq // STEP_ROWS),
        grid=(t // STEP_ROWS,),
        in_specs=[
            _step_rows(), _step_rows(),
            _resident((1, D_MODEL)),
            _resident((D_MODEL, CONV_WIDTH), (0, 0)),
            _resident((D_MODEL, CONV_WIDTH), (0, 1)),
            _resident((D_MODEL, CONV_WIDTH), (0, 2)),
            _resident((SC_KERNEL, CONV_WIDTH)),
            _resident((CONV_WIDTH, D_MODEL), (0, 0)),
        ],
        out_specs=pl.BlockSpec((STEP_ROWS, D_MODEL), lambda s: (s, 0)),
        out_shape=jax.ShapeDtypeStruct((t, D_MODEL), F32),
        scratch_shapes=[pltpu.VMEM((CHUNK + SUBLANES, CONV_WIDTH), F32)],
        compiler_params=_params(1),
        name="conv_mix",
    )(h, h_res, norm_w, w_in, w_in, w_in, sc_w, w_out)


def _split_cat(a, axis, n_terms):
    terms = []
    rest = a
    for _ in range(n_terms):
        term = rest.astype(BF16)
        terms.append(term)
        rest = rest - term.astype(F32)
    return jnp.concatenate(terms, axis=axis)


def _ssd_chunk(xn, wz_ref, wx_ref, wbc_ref, wdt_ref, cwx_ref, cwbc_ref, cbx_ref, cbbc_ref,
               dtb_ref, alog_ref, dsk_ref, gnw_ref, e64_ref, xbuf, bcbuf, st_ref):
    xbuf[_BODY, :] = _proj(xn, wx_ref)
    xc = _silu(_causal_conv(xbuf, cwx_ref, SSD_CONV) + cbx_ref[...])
    xbuf[_HEAD, :] = xbuf[_TAIL, :]
    bcbuf[_BODY, :] = _proj(xn, wbc_ref)
    bcc = _silu(_causal_conv(bcbuf, cwbc_ref, SSD_CONV) + cbbc_ref[...])
    bcbuf[_HEAD, :] = bcbuf[_TAIL, :]

    dt_in = _proj(xn, wdt_ref) + dtb_ref[...]
    dtv = jnp.maximum(dt_in, 0.0) + jnp.log1p(jnp.exp(-jnp.abs(dt_in)))
    a_neg = -jnp.exp(alog_ref[...])
    row = lax.broadcasted_iota(jnp.int32, (CHUNK, CHUNK), 0)
    col = lax.broadcasted_iota(jnp.int32, (CHUNK, CHUNK), 1)
    causal = col <= row
    tril = causal.astype(BF16)
    acs = jnp.dot(jnp.concatenate([tril] * LOG_SPLIT, axis=1),
                  _split_cat(dtv * a_neg, 0, LOG_SPLIT),
                  preferred_element_type=F32)
    acs_t = acs.T
    decay = jnp.exp(acs[CHUNK - 1:CHUNK, :] - acs)
    exp_acs = jnp.exp(acs)

    per_head = jnp.concatenate([dtv, dtv * decay, exp_acs], axis=0)
    per_chan = jnp.dot(_split_cat(per_head, 1, SCALE_SPLIT), e64_ref[...],
                       preferred_element_type=F32)
    dt_e = per_chan[0:CHUNK]
    dt_dec_e = per_chan[CHUNK:2 * CHUNK]
    exp_acs_e = per_chan[2 * CHUNK:3 * CHUNK]
    chunk_decay = exp_acs_e[CHUNK - 1:CHUNK, :]

    x_dt_b = (xc * dt_e).astype(BF16)
    x_dec_b = (xc * dt_dec_e).astype(BF16)

    lane_head = lax.broadcasted_iota(jnp.int32, (CHUNK, QUAD_W), 1) // SSD_HEAD_DIM
    y_groups = []
    for g in range(SSD_GROUPS):
        gsl = slice(g * GROUP_W, (g + 1) * GROUP_W)
        b_g = bcc[:, g * SSD_STATE:(g + 1) * SSD_STATE]
        c_g = bcc[:, SSD_BC + g * SSD_STATE:SSD_BC + (g + 1) * SSD_STATE]
        b_gb = b_g.astype(BF16)
        c_gb = c_g.astype(BF16)
        cb = lax.dot_general(c_gb, b_gb, (((1,), (1,)), ((), ())),
                             preferred_element_type=F32)
        cb = jnp.where(causal, cb, 0.0)
        states = jnp.dot(b_g.T.astype(BF16), x_dec_b[:, gsl], preferred_element_type=F32)
        prev = st_ref[:, gsl]
        y_off = jnp.dot(c_gb, prev.astype(BF16), preferred_element_type=F32) * exp_acs_e[:, gsl]
        st_ref[:, gsl] = prev * chunk_decay[:, gsl] + states
        y_diag = []
        for q in range(HEADS_PER_GROUP // QUAD):
            h0 = g * HEADS_PER_GROUP + q * QUAD
            m_parts = []
            x_parts = []
            x_q = x_dt_b[:, h0 * SSD_HEAD_DIM:h0 * SSD_HEAD_DIM + QUAD_W]
            for hh in range(QUAD):
                h = h0 + hh
                diff = jnp.broadcast_to(acs[:, h:h + 1], (CHUNK, CHUNK)) - acs_t[h:h + 1, :]
                m_parts.append((cb * jnp.exp(jnp.where(causal, diff, 0.0))).astype(BF16))
                x_parts.append(jnp.where(lane_head == hh, x_q, jnp.zeros_like(x_q)))
            m_q = jnp.concatenate(m_parts, axis=1)
            x_bd = jnp.concatenate(x_parts, axis=0)
            y_diag.append(jnp.dot(m_q, x_bd, preferred_element_type=F32))
        y_groups.append(jnp.concatenate(y_diag, axis=1) + y_off)
    y = jnp.concatenate(y_groups, axis=1) + dsk_ref[...] * xc

    gz = y * _silu(_proj(xn, wz_ref))
    gnw = gnw_ref[...]
    out = []
    for g in range(SSD_GROUPS):
        gsl = slice(g * GROUP_W, (g + 1) * GROUP_W)
        gg = gz[:, gsl]
        out.append(gg * lax.rsqrt(jnp.mean(gg * gg, axis=-1, keepdims=True) + EPS) * gnw[:, gsl])
    return jnp.concatenate(out, axis=1)


def _ssd_mix_body(x_ref, nw_ref, wz_ref, wx_ref, wbc_ref, wdt_ref,
                  cwx_ref, cwbc_ref, cbx_ref, cbbc_ref,
                  dtb_ref, alog_ref, dsk_ref, gnw_ref, e64_ref, wout_ref, o_ref,
                  xbuf, bcbuf, st_ref, *, steps_per_seq):
    @pl.when(pl.program_id(0) % steps_per_seq == 0)
    def _():
        xbuf[_HEAD, :] = jnp.zeros((SUBLANES, SSD_WIDTH), F32)
        bcbuf[_HEAD, :] = jnp.zeros((SUBLANES, 2 * SSD_BC), F32)
        st_ref[...] = jnp.zeros_like(st_ref)

    for r in range(0, STEP_ROWS, CHUNK):
        rows = pl.ds(r, CHUNK)
        xn = _rmsnorm(x_ref[rows, :], nw_ref[...]).astype(BF16)
        y_b = _ssd_chunk(xn, wz_ref, wx_ref, wbc_ref, wdt_ref, cwx_ref, cwbc_ref, cbx_ref,
                         cbbc_ref, dtb_ref, alog_ref, dsk_ref, gnw_ref, e64_ref,
                         xbuf, bcbuf, st_ref)
        o_ref[rows, :] = x_ref[rows, :] + jnp.dot(
            y_b.astype(BF16), wout_ref[...], preferred_element_type=F32)


def _ssd_mix(h, norm_w, w_in, w_dt, cw_x, cw_bc, cb_x, cb_bc, dt_bias, a_log, d_skip,
             gnorm_w, e64, w_out, *, seq):
    t = h.shape[0]
    return pl.pallas_call(
        functools.partial(_ssd_mix_body, steps_per_seq=seq // STEP_ROWS),
        grid=(t // STEP_ROWS,),
        in_specs=[
            _step_rows(),
            _resident((1, D_MODEL)),
            _resident((D_MODEL, SSD_WIDTH), (0, CONV_IN_COLS // SSD_WIDTH)),
            _resident((D_MODEL, SSD_WIDTH), (0, CONV_IN_COLS // SSD_WIDTH + 1)),
            _resident((D_MODEL, 2 * SSD_BC), (0, (CONV_IN_COLS + 2 * SSD_WIDTH) // (2 * SSD_BC))),
            _resident((D_MODEL, LANES)),
            _resident((SSD_CONV, SSD_WIDTH)),
            _resident((SSD_CONV, 2 * SSD_BC)),
            _resident((1, SSD_WIDTH)),
            _resident((1, 2 * SSD_BC)),
            _resident((1, LANES)),
            _resident((1, LANES)),
            _resident((1, SSD_WIDTH)),
            _resident((1, SSD_WIDTH)),
            _resident((SCALE_SPLIT * LANES, SSD_WIDTH)),
            _resident((SSD_WIDTH, D_MODEL), (1, 0)),
        ],
        out_specs=pl.BlockSpec((STEP_ROWS, D_MODEL), lambda s: (s, 0)),
        out_shape=jax.ShapeDtypeStruct((t, D_MODEL), F32),
        scratch_shapes=[
            pltpu.VMEM((CHUNK + SUBLANES, SSD_WIDTH), F32),
            pltpu.VMEM((CHUNK + SUBLANES, 2 * SSD_BC), F32),
            pltpu.VMEM((SSD_STATE, SSD_WIDTH), F32),
        ],
        compiler_params=_params(1),
        name="ssd_mix",
    )(h, norm_w, w_in, w_in, w_in, w_dt, cw_x, cw_bc, cb_x, cb_bc, dt_bias, a_log,
      d_skip, gnorm_w, e64, w_out)


def _transpose_cast_body(w_ref, o_ref):
    o_ref[...] = w_ref[...].T.astype(BF16)


def _transpose_cast(w_t, *, n_rows, bn):
    d = w_t.shape[1]
    return pl.pallas_call(
        _transpose_cast_body,
        grid=(n_rows // bn,),
        in_specs=[pl.BlockSpec((bn, d), lambda j: (j, 0))],
        out_specs=pl.BlockSpec((d, bn), lambda j: (0, j)),
        out_shape=jax.ShapeDtypeStruct((d, n_rows), BF16),
        compiler_params=_params(1),
        name="transpose_cast",
    )(w_t)


def _ple_final_body(h_ref, p_ref, pnw_ref, wg_ref, wp_ref, fnw_ref, o_ref, *, last_layer):
    h = h_ref[...]
    xn = _rmsnorm(h, pnw_ref[...]).astype(BF16)
    gate = jax.nn.sigmoid(jnp.dot(xn, wg_ref[...], preferred_element_type=F32))
    proj = jnp.dot(p_ref[...].astype(BF16), wp_ref[...], preferred_element_type=F32)
    h = h + gate * proj
    o_ref[...] = _rmsnorm(h, fnw_ref[...]) if last_layer else h


def _ple_final(h, p, ple_norm, w_gate, w_proj, final_norm, *, tm, last_layer):
    t = h.shape[0]
    ple_dim = p.shape[1]
    return pl.pallas_call(
        functools.partial(_ple_final_body, last_layer=last_layer),
        grid=(t // tm,),
        in_specs=[
            pl.BlockSpec((tm, D_MODEL), lambda i: (i, 0)),
            pl.BlockSpec((tm, ple_dim), lambda i: (i, 0)),
            pl.BlockSpec((1, D_MODEL), lambda i: (0, 0)),
            pl.BlockSpec((D_MODEL, D_MODEL), lambda i: (0, 0)),
            pl.BlockSpec((ple_dim, D_MODEL), lambda i: (0, 0)),
            pl.BlockSpec((1, D_MODEL), lambda i: (0, 0)),
        ],
        out_specs=pl.BlockSpec((tm, D_MODEL), lambda i: (i, 0)),
        out_shape=jax.ShapeDtypeStruct((t, D_MODEL), F32),
        compiler_params=_params(1),
        name="ple_final",
    )(h, p, ple_norm, w_gate, w_proj, final_norm)


def _head_selector(width, n_terms):
    src = jnp.arange(LANES, dtype=jnp.int32)[:, None]
    dst = jnp.arange(SSD_HEADS * width, dtype=jnp.int32)[None, :] // width
    return jnp.tile((src == dst).astype(BF16), (n_terms, 1))


def _pad_lanes(v):
    return jnp.pad(v.astype(F32), (0, LANES - v.shape[0]))[None, :]


def kernel(x, p, ffn1_norm, ffn1_w_in, ffn1_w_out, mix_norm, mix_w_in, sc_conv_w, ssd_conv_w,
           ssd_conv_b, ssd_dt_bias, ssd_a_log, ssd_d, ssd_norm, mix_w_out, ffn2_norm, ffn2_w_in,
           ffn2_w_out, ple_norm, ple_w_gate, ple_w_proj, final_norm):
    batch, seq, _ = x.shape
    depth = ffn1_norm.shape[0]
    t = batch * seq
    h = x.reshape(t, D_MODEL)
    e64 = _head_selector(SSD_HEAD_DIM, SCALE_SPLIT)
    row = lambda v: v.astype(F32)[None, :]

    for i in range(depth):
        w_in_t = jnp.swapaxes(mix_w_in, 1, 2)[i]
        h, w_in, w_out = _ffn(h, row(ffn1_norm[i]), ffn1_w_in[i], ffn1_w_out[i],
                              tm=1024, tf=512, name="ffn1", transpose_jobs=[w_in_t],
                              cast_jobs=[(mix_w_out[i], 64)])
        assert w_in.shape[1] == MAIN_COLS
        w_dt = _transpose_cast(jnp.pad(w_in_t[MAIN_COLS:], ((0, LANES - SSD_HEADS), (0, 0))),
                               n_rows=LANES, bn=LANES)
        cw = ssd_conv_w[i].astype(F32)
        cb = ssd_conv_b[i].astype(F32)[None, :]
        h_ssd = _ssd_mix(h, row(mix_norm[i]), w_in, w_dt,
                         cw[:, :SSD_WIDTH], cw[:, SSD_WIDTH:], cb[:, :SSD_WIDTH], cb[:, SSD_WIDTH:],
                         _pad_lanes(ssd_dt_bias[i]), _pad_lanes(ssd_a_log[i]),
                         jnp.repeat(ssd_d[i].astype(F32), SSD_HEAD_DIM)[None, :],
                         row(ssd_norm[i]), e64, w_out, seq=seq)
        h = _conv_mix(h, h_ssd, row(mix_norm[i]), w_in, sc_conv_w[i].astype(F32), w_out, seq=seq)

        h, ple_gate_w = _ffn(h, row(ffn2_norm[i]), ffn2_w_in[i], ffn2_w_out[i],
                             tm=1024, tf=512, name="ffn2", cast_jobs=[(ple_w_gate[i], 32)])

        h = _ple_final(h, p[i].reshape(t, -1), row(ple_norm[i]), ple_gate_w,
                       ple_w_proj[i].astype(BF16), row(final_norm), tm=512,
                       last_layer=(i + 1 == depth))
    return h.reshape(batch, seq, D_MODEL)
```

```python
import functools

import jax
import jax.numpy as jnp
from jax import lax
from jax.experimental import pallas as pl
from jax.experimental.pallas import tpu as pltpu

F32 = jnp.float32
BF16 = jnp.bfloat16

D_MODEL = 2048
D_FF = 5632
CONV_WIDTH = 2048
SC_KERNEL = 3
SSD_HEADS = 32
SSD_HEAD_DIM = 64
SSD_WIDTH = SSD_HEADS * SSD_HEAD_DIM
SSD_STATE = 128
SSD_GROUPS = 4
SSD_CONV = 4
CHUNK = 128
SSD_BC = SSD_GROUPS * SSD_STATE
SSD_XBC = SSD_WIDTH + 2 * SSD_BC
EPS = 1e-6

LANES = 128
SUBLANES = 8
VMEM_LIMIT_BYTES = 60000 * 1024
FFN_VMEM_LIMIT_BYTES = 127 * 512 * 1024
GROUP_W = SSD_WIDTH // SSD_GROUPS
HEADS_PER_GROUP = SSD_HEADS // SSD_GROUPS
QUAD = 4
QUAD_W = QUAD * SSD_HEAD_DIM
LOG_SPLIT = 3
SCALE_SPLIT = 2

CONV_IN_COLS = 3 * CONV_WIDTH
MAIN_COLS = CONV_IN_COLS + SSD_WIDTH + SSD_XBC
STEP_ROWS = 2 * CHUNK


def _params(n_axes, vmem_limit_bytes=VMEM_LIMIT_BYTES):
    return pltpu.CompilerParams(
        dimension_semantics=("arbitrary",) * n_axes,
        vmem_limit_bytes=vmem_limit_bytes)


def _rmsnorm(x, w):
    return x * lax.rsqrt(jnp.mean(x * x, axis=-1, keepdims=True) + EPS) * w


def _silu(x):
    return x * jax.nn.sigmoid(x)


def _resident(shape, block_index=None):
    index = (0,) * len(shape) if block_index is None else block_index
    return pl.BlockSpec(shape, lambda *_: index, pipeline_mode=pl.Buffered(1))


def _ffn_body(x_hbm, nw_ref, wg_ref, wu_ref, wo_ref, *rest, tm, n_transpose, n_cast):
    n_side = n_transpose + n_cast
    side_in = rest[:n_side]
    o_ref = rest[n_side]
    side_out = rest[n_side + 1:2 * n_side + 1]
    xn_ref, xbuf, sem = rest[2 * n_side + 1:]
    i = pl.program_id(0)
    j = pl.program_id(1)

    def x_copy(tile):
        return pltpu.make_async_copy(x_hbm.at[pl.ds(tile * tm, tm), :], xbuf, sem)

    @pl.when((i == 0) & (j == 0))
    def _():
        x_copy(0).start()

    @pl.when(j == 0)
    def _():
        x_copy(i).wait()
        nw = nw_ref[...]

        def slab(r, carry):
            rows = pl.ds(pl.multiple_of(r * NORM_SLAB, NORM_SLAB), NORM_SLAB)
            x = xbuf[rows, :]
            xn_ref[rows, :] = _rmsnorm(x, nw).astype(BF16)
            o_ref[rows, :] = x
            return carry

        lax.fori_loop(0, tm // NORM_SLAB, slab, None)

    @pl.when((j == 1) & (i + 1 < pl.num_programs(0)))
    def _():
        x_copy(i + 1).start()

    for k in range(n_side):
        block = side_in[k][...]
        side_out[k][...] = (block.T if k < n_transpose else block).astype(BF16)
    xn = xn_ref[...]
    g = jnp.dot(xn, wg_ref[...].astype(BF16), preferred_element_type=F32)
    u = jnp.dot(xn, wu_ref[...].astype(BF16), preferred_element_type=F32)
    a = (0.5 * _silu(g) * u).astype(BF16)
    o_ref[...] += jnp.dot(a, wo_ref[...].astype(BF16), preferred_element_type=F32)


SIDE_ROWS = 128
NORM_SLAB = 256


def _ffn(h, norm_w, w_in, w_out, *, tm, tf, name, transpose_jobs=(), cast_jobs=()):
    t = h.shape[0]
    nj = D_FF // tf
    assert nj >= 2, "the x prefetch is issued in the second d_ff step"
    in_specs = [
        pl.BlockSpec(memory_space=pl.ANY),
        pl.BlockSpec((1, D_MODEL), lambda i, j: (0, 0)),
        pl.BlockSpec((D_MODEL, tf), lambda i, j: (0, j)),
        pl.BlockSpec((D_MODEL, tf), lambda i, j: (0, j + nj)),
        pl.BlockSpec((tf, D_MODEL), lambda i, j: (j, 0)),
    ]
    out_specs = [pl.BlockSpec((tm, D_MODEL), lambda i, j: (i, 0))]
    out_shape = [jax.ShapeDtypeStruct((t, D_MODEL), F32)]
    operands = [h, norm_w, w_in, w_in, w_out]
    n_steps = (t // tm) * nj
    for w_t in transpose_jobs:
        assert n_steps * SIDE_ROWS <= w_t.shape[0]
        in_specs.append(pl.BlockSpec((SIDE_ROWS, D_MODEL), lambda i, j: (i * nj + j, 0)))
        out_specs.append(pl.BlockSpec((D_MODEL, SIDE_ROWS), lambda i, j: (0, i * nj + j)))
        out_shape.append(jax.ShapeDtypeStruct((D_MODEL, n_steps * SIDE_ROWS), BF16))
        operands.append(w_t)
    for w, rows in cast_jobs:
        last = w.shape[0] // rows - 1
        assert w.shape[0] % rows == 0 and last < n_steps
        spec = pl.BlockSpec((rows, w.shape[1]),
                            lambda i, j, last=last: (jnp.minimum(i * nj + j, last), 0))
        in_specs.append(spec)
        out_specs.append(spec)
        out_shape.append(jax.ShapeDtypeStruct(w.shape, BF16))
        operands.append(w)
    return pl.pallas_call(
        functools.partial(_ffn_body, tm=tm, n_transpose=len(transpose_jobs),
                          n_cast=len(cast_jobs)),
        grid=(t // tm, nj),
        in_specs=in_specs,
        out_specs=out_specs,
        out_shape=out_shape,
        scratch_shapes=[
            pltpu.VMEM((tm, D_MODEL), BF16),
            pltpu.VMEM((tm, D_MODEL), F32),
            pltpu.SemaphoreType.DMA(()),
        ],
        compiler_params=_params(2, FFN_VMEM_LIMIT_BYTES),
        name=name,
    )(*operands)


def _causal_conv(buf_ref, w_ref, n_taps):
    ext = buf_ref[...]
    acc = None
    for k in range(n_taps):
        delay = n_taps - 1 - k
        shifted = ext if delay == 0 else pltpu.roll(ext, delay, 0)
        tap = shifted[SUBLANES:, :] * w_ref[k:k + 1, :]
        acc = tap if acc is None else acc + tap
    return acc


_BODY = pl.ds(SUBLANES, CHUNK)
_TAIL = pl.ds(CHUNK, SUBLANES)
_HEAD = pl.ds(0, SUBLANES)


def _proj(xn, w_ref):
    return jnp.dot(xn, w_ref[...], preferred_element_type=F32)


def _step_rows():
    return pl.BlockSpec((STEP_ROWS, D_MODEL), lambda s: (s, 0))


def _conv_mix_body(x_ref, res_ref, nw_ref, wb_ref, wc_ref, wx_ref, scw_ref, wout_ref, o_ref,
                   pbuf, *, steps_per_seq):
    @pl.when(pl.program_id(0) % steps_per_seq == 0)
    def _():
        pbuf[_HEAD, :] = jnp.zeros((SUBLANES, CONV_WIDTH), F32)

    for r in range(0, STEP_ROWS, CHUNK):
        rows = pl.ds(r, CHUNK)
        xn = _rmsnorm(x_ref[rows, :], nw_ref[...]).astype(BF16)
        pbuf[_BODY, :] = _proj(xn, wc_ref) * _proj(xn, wx_ref)
        y_a = _proj(xn, wb_ref) * _causal_conv(pbuf, scw_ref, SC_KERNEL)
        pbuf[_HEAD, :] = pbuf[_TAIL, :]
        o_ref[rows, :] = res_ref[rows, :] + jnp.dot(
            y_a.astype(BF16), wout_ref[...], preferred_element_type=F32)


def _conv_mix(h, h_res, norm_w, w_in, sc_w, w_out, *, seq):
    t = h.shape[0]
    return pl.pallas_call(
        functools.partial(_conv_mix_body, steps_per_seq=seq // STEP_ROWS),
        grid=(t // STEP_ROWS,),
        in_specs=[
            _step_rows(), _step_rows(),
            _resident((1, D_MODEL)),
            _resident((D_MODEL, CONV_WIDTH), (0, 0)),
            _resident((D_MODEL, CONV_WIDTH), (0, 1)),
            _resident((D_MODEL, CONV_WIDTH), (0, 2)),
            _resident((SC_KERNEL, CONV_WIDTH)),
            _resident((CONV_WIDTH, D_MODEL), (0, 0)),
        ],
        out_specs=pl.BlockSpec((STEP_ROWS, D_MODEL), lambda s: (s, 0)),
        out_shape=jax.ShapeDtypeStruct((t, D_MODEL), F32),
        scratch_shapes=[pltpu.VMEM((CHUNK + SUBLANES, CONV_WIDTH), F32)],
        compiler_params=_params(1),
        name="conv_mix",
    )(h, h_res, norm_w, w_in, w_in, w_in, sc_w, w_out)


def _split_cat(a, axis, n_terms):
    terms = []
    rest = a
    for _ in range(n_terms):
        term = rest.astype(BF16)
        terms.append(term)
        rest = rest - term.astype(F32)
    return jnp.concatenate(terms, axis=axis)


def _ssd_chunk(xn, wz_ref, wx_ref, wbc_ref, wdt_ref, cwx_ref, cwbc_ref, cbx_ref, cbbc_ref,
               dtb_ref, alog_ref, dsk_ref, gnw_ref, e64_ref, xbuf, bcbuf, st_ref):
    xbuf[_BODY, :] = _proj(xn, wx_ref)
    xc = _silu(_causal_conv(xbuf, cwx_ref, SSD_CONV) + cbx_ref[...])
    xbuf[_HEAD, :] = xbuf[_TAIL, :]
    bcbuf[_BODY, :] = _proj(xn, wbc_ref)
    bcc = _silu(_causal_conv(bcbuf, cwbc_ref, SSD_CONV) + cbbc_ref[...])
    bcbuf[_HEAD, :] = bcbuf[_TAIL, :]

    dt_in = _proj(xn, wdt_ref) + dtb_ref[...]
    dtv = jnp.maximum(dt_in, 0.0) + jnp.log1p(jnp.exp(-jnp.abs(dt_in)))
    a_neg = -jnp.exp(alog_ref[...])
    row = lax.broadcasted_iota(jnp.int32, (CHUNK, CHUNK), 0)
    col = lax.broadcasted_iota(jnp.int32, (CHUNK, CHUNK), 1)
    causal = col <= row
    tril = causal.astype(BF16)
    acs = jnp.dot(jnp.concatenate([tril] * LOG_SPLIT, axis=1),
                  _split_cat(dtv * a_neg, 0, LOG_SPLIT),
                  preferred_element_type=F32)
    acs_t = acs.T
    decay = jnp.exp(acs[CHUNK - 1:CHUNK, :] - acs)
    exp_acs = jnp.exp(acs)

    per_head = jnp.concatenate([dtv, dtv * decay, exp_acs], axis=0)
    per_chan = jnp.dot(_split_cat(per_head, 1, SCALE_SPLIT), e64_ref[...],
                       preferred_element_type=F32)
    dt_e = per_chan[0:CHUNK]
    dt_dec_e = per_chan[CHUNK:2 * CHUNK]
    exp_acs_e = per_chan[2 * CHUNK:3 * CHUNK]
    chunk_decay = exp_acs_e[CHUNK - 1:CHUNK, :]

    x_dt_b = (xc * dt_e).astype(BF16)
    x_dec_b = (xc * dt_dec_e).astype(BF16)

    lane_head = lax.broadcasted_iota(jnp.int32, (CHUNK, QUAD_W), 1) // SSD_HEAD_DIM
    y_groups = []
    for g in range(SSD_GROUPS):
        gsl = slice(g * GROUP_W, (g + 1) * GROUP_W)
        b_g = bcc[:, g * SSD_STATE:(g + 1) * SSD_STATE]
        c_g = bcc[:, SSD_BC + g * SSD_STATE:SSD_BC + (g + 1) * SSD_STATE]
        b_gb = b_g.astype(BF16)
        c_gb = c_g.astype(BF16)
        cb = lax.dot_general(c_gb, b_gb, (((1,), (1,)), ((), ())),
                             preferred_element_type=F32)
        cb = jnp.where(causal, cb, 0.0)
        states = jnp.dot(b_g.T.astype(BF16), x_dec_b[:, gsl], preferred_element_type=F32)
        prev = st_ref[:, gsl]
        y_off = jnp.dot(c_gb, prev.astype(BF16), preferred_element_type=F32) * exp_acs_e[:, gsl]
        st_ref[:, gsl] = prev * chunk_decay[:, gsl] + states
        y_diag = []
        for q in range(HEADS_PER_GROUP // QUAD):
            h0 = g * HEADS_PER_GROUP + q * QUAD
            m_parts = []
            x_parts = []
            x_q = x_dt_b[:, h0 * SSD_HEAD_DIM:h0 * SSD_HEAD_DIM + QUAD_W]
            for hh in range(QUAD):
                h = h0 + hh
                diff = jnp.broadcast_to(acs[:, h:h + 1], (CHUNK, CHUNK)) - acs_t[h:h + 1, :]
                m_parts.append((cb * jnp.exp(jnp.where(causal, diff, 0.0))).astype(BF16))
                x_parts.append(jnp.where(lane_head == hh, x_q, jnp.zeros_like(x_q)))
            m_q = jnp.concatenate(m_parts, axis=1)
            x_bd = jnp.concatenate(x_parts, axis=0)
            y_diag.append(jnp.dot(m_q, x_bd, preferred_element_type=F32))
        y_groups.append(jnp.concatenate(y_diag, axis=1) + y_off)
    y = jnp.concatenate(y_groups, axis=1) + dsk_ref[...] * xc

    gz = y * _silu(_proj(xn, wz_ref))
    gnw = gnw_ref[...]
    out = []
    for g in range(SSD_GROUPS):
        gsl = slice(g * GROUP_W, (g + 1) * GROUP_W)
        gg = gz[:, gsl]
        out.append(gg * lax.rsqrt(jnp.mean(gg * gg, axis=-1, keepdims=True) + EPS) * gnw[:, gsl])
    return jnp.concatenate(out, axis=1)


def _ssd_mix_body(x_ref, nw_ref, wz_ref, wx_ref, wbc_ref, wdt_ref,
                  cwx_ref, cwbc_ref, cbx_ref, cbbc_ref,
                  dtb_ref, alog_ref, dsk_ref, gnw_ref, e64_ref, wout_ref, o_ref,
                  xbuf, bcbuf, st_ref, *, steps_per_seq):
    @pl.when(pl.program_id(0) % steps_per_seq == 0)
    def _():
        xbuf[_HEAD, :] = jnp.zeros((SUBLANES, SSD_WIDTH), F32)
        bcbuf[_HEAD, :] = jnp.zeros((SUBLANES, 2 * SSD_BC), F32)
        st_ref[...] = jnp.zeros_like(st_ref)

    for r in range(0, STEP_ROWS, CHUNK):
        rows = pl.ds(r, CHUNK)
        xn = _rmsnorm(x_ref[rows, :], nw_ref[...]).astype(BF16)
        y_b = _ssd_chunk(xn, wz_ref, wx_ref, wbc_ref, wdt_ref, cwx_ref, cwbc_ref, cbx_ref,
                         cbbc_ref, dtb_ref, alog_ref, dsk_ref, gnw_ref, e64_ref,
                         xbuf, bcbuf, st_ref)
        o_ref[rows, :] = x_ref[rows, :] + jnp.dot(
            y_b.astype(BF16), wout_ref[...], preferred_element_type=F32)


def _ssd_mix(h, norm_w, w_in, w_dt, cw_x, cw_bc, cb_x, cb_bc, dt_bias, a_log, d_skip,
             gnorm_w, e64, w_out, *, seq):
    t = h.shape[0]
    return pl.pallas_call(
        functools.partial(_ssd_mix_body, steps_per_seq=seq // STEP_ROWS),
        grid=(t // STEP_ROWS,),
        in_specs=[
            _step_rows(),
            _resident((1, D_MODEL)),
            _resident((D_MODEL, SSD_WIDTH), (0, CONV_IN_COLS // SSD_WIDTH)),
            _resident((D_MODEL, SSD_WIDTH), (0, CONV_IN_COLS // SSD_WIDTH + 1)),
            _resident((D_MODEL, 2 * SSD_BC), (0, (CONV_IN_COLS + 2 * SSD_WIDTH) // (2 * SSD_BC))),
            _resident((D_MODEL, LANES)),
            _resident((SSD_CONV, SSD_WIDTH)),
            _resident((SSD_CONV, 2 * SSD_BC)),
            _resident((1, SSD_WIDTH)),
            _resident((1, 2 * SSD_BC)),
            _resident((1, LANES)),
            _resident((1, LANES)),
            _resident((1, SSD_WIDTH)),
            _resident((1, SSD_WIDTH)),
            _resident((SCALE_SPLIT * LANES, SSD_WIDTH)),
            _resident((SSD_WIDTH, D_MODEL), (1, 0)),
        ],
        out_specs=pl.BlockSpec((STEP_ROWS, D_MODEL), lambda s: (s, 0)),
        out_shape=jax.ShapeDtypeStruct((t, D_MODEL), F32),
        scratch_shapes=[
            pltpu.VMEM((CHUNK + SUBLANES, SSD_WIDTH), F32),
            pltpu.VMEM((CHUNK + SUBLANES, 2 * SSD_BC), F32),
            pltpu.VMEM((SSD_STATE, SSD_WIDTH), F32),
        ],
        compiler_params=_params(1),
        name="ssd_mix",
    )(h, norm_w, w_in, w_in, w_in, w_dt, cw_x, cw_bc, cb_x, cb_bc, dt_bias, a_log,
      d_skip, gnorm_w, e64, w_out)


def _transpose_cast_body(w_ref, o_ref):
    o_ref[...] = w_ref[...].T.astype(BF16)


def _transpose_cast(w_t, *, n_rows, bn):
    d = w_t.shape[1]
    return pl.pallas_call(
        _transpose_cast_body,
        grid=(n_rows // bn,),
        in_specs=[pl.BlockSpec((bn, d), lambda j: (j, 0))],
        out_specs=pl.BlockSpec((d, bn), lambda j: (0, j)),
        out_shape=jax.ShapeDtypeStruct((d, n_rows), BF16),
        compiler_params=_params(1),
        name="transpose_cast",
    )(w_t)


def _ple_final_body(h_ref, p_ref, pnw_ref, wg_ref, wp_ref, fnw_ref, o_ref, *, last_layer):
    h = h_ref[...]
    xn = _rmsnorm(h, pnw_ref[...]).astype(BF16)
    gate = jax.nn.sigmoid(jnp.dot(xn, wg_ref[...], preferred_element_type=F32))
    proj = jnp.dot(p_ref[...].astype(BF16), wp_ref[...], preferred_element_type=F32)
    h = h + gate * proj
    o_ref[...] = _rmsnorm(h, fnw_ref[...]) if last_layer else h


def _ple_final(h, p, ple_norm, w_gate, w_proj, final_norm, *, tm, last_layer):
    t = h.shape[0]
    ple_dim = p.shape[1]
    return pl.pallas_call(
        functools.partial(_ple_final_body, last_layer=last_layer),
        grid=(t // tm,),
        in_specs=[
            pl.BlockSpec((tm, D_MODEL), lambda i: (i, 0)),
            pl.BlockSpec((tm, ple_dim), lambda i: (i, 0)),
            _resident((1, D_MODEL)),
            _resident((D_MODEL, D_MODEL)),
            _resident((ple_dim, D_MODEL)),
            _resident((1, D_MODEL)),
        ],
        out_specs=pl.BlockSpec((tm, D_MODEL), lambda i: (i, 0)),
        out_shape=jax.ShapeDtypeStruct((t, D_MODEL), F32),
        compiler_params=_params(1),
        name="ple_final",
    )(h, p, ple_norm, w_gate, w_proj, final_norm)


def _head_selector(width, n_terms):
    src = jnp.arange(LANES, dtype=jnp.int32)[:, None]
    dst = jnp.arange(SSD_HEADS * width, dtype=jnp.int32)[None, :] // width
    return jnp.tile((src == dst).astype(BF16), (n_terms, 1))


def _pad_lanes(v):
    return jnp.pad(v.astype(F32), (0, LANES - v.shape[0]))[None, :]


def kernel(x, p, ffn1_norm, ffn1_w_in, ffn1_w_out, mix_norm, mix_w_in, sc_conv_w, ssd_conv_w,
           ssd_conv_b, ssd_dt_bias, ssd_a_log, ssd_d, ssd_norm, mix_w_out, ffn2_norm, ffn2_w_in,
           ffn2_w_out, ple_norm, ple_w_gate, ple_w_proj, final_norm):
    batch, seq, _ = x.shape
    depth = ffn1_norm.shape[0]
    t = batch * seq
    h = x.reshape(t, D_MODEL)
    e64 = _head_selector(SSD_HEAD_DIM, SCALE_SPLIT)
    row = lambda v: v.astype(F32)[None, :]

    for i in range(depth):
        w_in_t = jnp.swapaxes(mix_w_in, 1, 2)[i]
        h, w_in, w_out = _ffn(h, row(ffn1_norm[i]), ffn1_w_in[i], ffn1_w_out[i],
                              tm=1024, tf=512, name="ffn1", transpose_jobs=[w_in_t],
                              cast_jobs=[(mix_w_out[i], 64)])
        assert w_in.shape[1] == MAIN_COLS
        w_dt = _transpose_cast(jnp.pad(w_in_t[MAIN_COLS:], ((0, LANES - SSD_HEADS), (0, 0))),
                               n_rows=LANES, bn=LANES)
        cw = ssd_conv_w[i].astype(F32)
        cb = ssd_conv_b[i].astype(F32)[None, :]
        h_ssd = _ssd_mix(h, row(mix_norm[i]), w_in, w_dt,
                         cw[:, :SSD_WIDTH], cw[:, SSD_WIDTH:], cb[:, :SSD_WIDTH], cb[:, SSD_WIDTH:],
                         _pad_lanes(ssd_dt_bias[i]), _pad_lanes(ssd_a_log[i]),
                         jnp.repeat(ssd_d[i].astype(F32), SSD_HEAD_DIM)[None, :],
                         row(ssd_norm[i]), e64, w_out, seq=seq)
        h = _conv_mix(h, h_ssd, row(mix_norm[i]), w_in, sc_conv_w[i].astype(F32), w_out, seq=seq)

        h, ple_gate_w = _ffn(h, row(ffn2_norm[i]), ffn2_w_in[i], ffn2_w_out[i],
                             tm=1024, tf=512, name="ffn2", cast_jobs=[(ple_w_gate[i], 32)])

        h = _ple_final(h, p[i].reshape(t, -1), row(ple_norm[i]), ple_gate_w,
                       ple_w_proj[i].astype(BF16), row(final_norm), tm=1024,
                       last_layer=(i + 1 == depth))
    return h.reshape(batch, seq, D_MODEL)
```
